```python
import jax, jax.numpy as jnp
from jax import lax
import numpy as np

D_MODEL = 1024
BATCH = 2
SEQ = 8192
DEPTH = 1

D_MIX = D_MODEL
GLA_WIDTH = D_MIX // 2
GLA_HEADS = 4
GLA_DV = GLA_WIDTH // GLA_HEADS
GLA_DK = GLA_DV // 2
GLA_KW = GLA_HEADS * GLA_DK
GLA_GATE_RANK = 16
GLA_TAU = 16.0
GLA_CHUNK = 64
ATT_WIDTH = D_MIX - GLA_WIDTH
ATT_HEADS = 8
ATT_HD = ATT_WIDTH // ATT_HEADS
ROT_DIM = ATT_HD // 4
ROPE_THETA = 500000.0
DILATED_PATTERNS = ((128, 1), (512, 4), (2048, 16))
ATT_BLOCK = 128
D_FF = 2816
EPS = 1e-6
IN_SIZES = (GLA_KW, GLA_KW, GLA_WIDTH, GLA_WIDTH, GLA_GATE_RANK, ATT_WIDTH, ATT_WIDTH, ATT_WIDTH)
D_IN = GLA_KW * 2 + GLA_WIDTH * 2 + GLA_GATE_RANK + ATT_WIDTH * 3

kernel_name = "hymba_gla_dilated_macaron_layer"


def rms_norm(x, g):
    xf = x.astype(jnp.float32)
    y = xf * lax.rsqrt(jnp.mean(xf * xf, axis=-1, keepdims=True) + EPS)
    return (y * g.astype(jnp.float32)).astype(x.dtype)


def swiglu(x, w1, w3, w2):
    return (jax.nn.silu(x @ w1) * (x @ w3)) @ w2


def rope_tables(positions):
    inv_freq = ROPE_THETA ** (-jnp.arange(0, ROT_DIM, 2, dtype=jnp.float32) / ROT_DIM)
    ang = positions.astype(jnp.float32)[..., None] * inv_freq
    return jnp.cos(ang)[:, :, None, :], jnp.sin(ang)[:, :, None, :]


def rope_partial(t, cos, sin):
    half = ROT_DIM // 2
    cos = cos.astype(t.dtype)
    sin = sin.astype(t.dtype)
    t1 = t[..., :half]
    t2 = t[..., half:ROT_DIM]
    return jnp.concatenate([t1 * cos - t2 * sin, t2 * cos + t1 * sin, t[..., ROT_DIM:]], axis=-1)


def gla_chunked(q, k, v, log_a):
    B, H, S, dk = q.shape
    dv = v.shape[-1]
    n = S // GLA_CHUNK

    def chunks(t):
        return t.astype(jnp.float32).reshape(B, H, n, GLA_CHUNK, t.shape[-1])

    q, k, v, g = chunks(q), chunks(k), chunks(v), chunks(log_a)
    b = jnp.cumsum(g, axis=3)
    b_last = b[:, :, :, -1:, :]
    q_dec = q * jnp.exp(b)
    k_inv = k * jnp.exp(-b)
    k_tail = k * jnp.exp(b_last - b)
    causal = jnp.tril(jnp.ones((GLA_CHUNK, GLA_CHUNK), dtype=bool))
    a = jnp.where(causal, jnp.einsum('bhnid,bhnjd->bhnij', q_dec, k_inv), 0.0)
    o_intra = jnp.einsum('bhnij,bhnje->bhnie', a, v)
    u = jnp.einsum('bhnjd,bhnje->bhnde', k_tail, v)
    decay = jnp.exp(b_last[:, :, :, 0, :])

    def step(state, inp):
        dec_c, u_c = inp
        return dec_c[..., None] * state + u_c, state

    init = jnp.zeros((B, H, dk, dv), jnp.float32)
    _, s_prev = lax.scan(step, init, (jnp.moveaxis(decay, 2, 0), jnp.moveaxis(u, 2, 0)))
    s_prev = jnp.moveaxis(s_prev, 0, 2)
    o_inter = jnp.einsum('bhnid,bhnde->bhnie', q_dec, s_prev)
    return (o_intra + o_inter).reshape(B, H, S, dv)


def dilated_branch(q, k, v, window, dilation):
    B, H, S, hd = q.shape
    L = S // dilation
    span = window // dilation
    nb = -(-L // ATT_BLOCK)
    Lp = nb * ATT_BLOCK

    def to_blocks(t):
        t = t.reshape(B, H, L, dilation, hd).transpose(0, 1, 3, 2, 4)
        t = jnp.pad(t, ((0, 0), (0, 0), (0, 0), (0, Lp - L), (0, 0)))
        return t.reshape(B, H, dilation, nb, ATT_BLOCK, hd)

    qb, kb, vb = to_blocks(q), to_blocks(k), to_blocks(v)
    shift = ((0, 0), (0, 0), (0, 0), (1, 0), (0, 0), (0, 0))
    kk = jnp.concatenate([jnp.pad(kb[:, :, :, :-1], shift), kb], axis=4)
    vv = jnp.concatenate([jnp.pad(vb[:, :, :, :-1], shift), vb], axis=4)
    s = jnp.einsum('bhrnqd,bhrnkd->bhrnqk', qb, kk).astype(jnp.float32)
    blk = jnp.arange(nb)[:, None, None]
    qi = jnp.arange(ATT_BLOCK)[None, :, None] + ATT_BLOCK
    ki = jnp.arange(2 * ATT_BLOCK)[None, None, :]
    dist = qi - ki
    mask = (dist >= 0) & (dist <= span) & (blk * ATT_BLOCK + ki - ATT_BLOCK >= 0)
    s = jnp.where(mask, s, -jnp.inf)
    m = jnp.max(s, axis=-1, keepdims=True)
    p = jnp.exp(s - m)
    den = jnp.sum(p, axis=-1, keepdims=True)
    o = jnp.einsum('bhrnqk,bhrnkd->bhrnqd', p.astype(v.dtype), vv).astype(jnp.float32) / den
    lse = (m + jnp.log(den))[..., 0]
    o = o.reshape(B, H, dilation, Lp, hd)[:, :, :, :L].transpose(0, 1, 3, 2, 4).reshape(B, H, S, hd)
    lse = lse.reshape(B, H, dilation, Lp)[:, :, :, :L].transpose(0, 1, 3, 2).reshape(B, H, S)
    return o, lse


def dilated_attention(q, k, v):
    outs, lses = [], []
    for window, dilation in DILATED_PATTERNS:
        o, lse = dilated_branch(q, k, v, window, dilation)
        outs.append(o)
        lses.append(lse)
    w = jax.nn.softmax(jnp.stack(lses, axis=0), axis=0)
    return jnp.sum(w[..., None] * jnp.stack(outs, axis=0), axis=0)


def setup_inputs(seed: int = 0) -> dict:
    key = jax.random.key(seed)
    ks = jax.random.split(key, 20)
    f32 = jnp.float32

    def nrm(k, shape, fan_in):
        return jax.random.normal(k, shape, f32) * (fan_in ** -0.5)

    def gain(k, shape):
        return 1.0 + 0.02 * jax.random.normal(k, shape, f32)

    x = jax.random.normal(ks[0], (BATCH, SEQ, D_MODEL), f32)
    positions = jnp.broadcast_to(jnp.arange(SEQ, dtype=jnp.int32)[None, :], (BATCH, SEQ))
    return {
        "x": x,
        "positions": positions,
        "ffn1_norm": gain(ks[1], (DEPTH, D_MODEL)),
        "ffn1_w1": nrm(ks[2], (DEPTH, D_MODEL, D_FF), D_MODEL),
        "ffn1_w3": nrm(ks[3], (DEPTH, D_MODEL, D_FF), D_MODEL),
        "ffn1_w2": nrm(ks[4], (DEPTH, D_FF, D_MODEL), D_FF),
        "mix_norm": gain(ks[5], (DEPTH, D_MODEL)),
        "w_in": nrm(ks[6], (DEPTH, D_MODEL, D_IN), D_MODEL),
        "gla_w_a2": nrm(ks[7], (DEPTH, GLA_GATE_RANK, GLA_KW), GLA_GATE_RANK),
        "gla_b_a": 0.01 * jax.random.normal(ks[8], (DEPTH, GLA_KW), f32),
        "gla_out_norm": gain(ks[9], (DEPTH, GLA_WIDTH)),
        "att_out_norm": gain(ks[10], (DEPTH, ATT_WIDTH)),
        "w_out": nrm(ks[11], (DEPTH, D_MIX, D_MODEL), D_MIX),
        "ffn2_norm": gain(ks[12], (DEPTH, D_MODEL)),
        "ffn2_w1": nrm(ks[13], (DEPTH, D_MODEL, D_FF), D_MODEL),
        "ffn2_w3": nrm(ks[14], (DEPTH, D_MODEL, D_FF), D_MODEL),
        "ffn2_w2": nrm(ks[15], (DEPTH, D_FF, D_MODEL), D_FF),
        "final_norm": gain(ks[16], (D_MODEL,)),
    }


def reference(x, positions, ffn1_norm, ffn1_w1, ffn1_w3, ffn1_w2, mix_norm, w_in, gla_w_a2, gla_b_a,
              gla_out_norm, att_out_norm, w_out, ffn2_norm, ffn2_w1, ffn2_w3, ffn2_w2, final_norm):
    B, S, _ = x.shape
    cos, sin = rope_tables(positions)
    offsets = []
    acc = 0
    for size in IN_SIZES[:-1]:
        acc += size
        offsets.append(acc)

    for l in range(DEPTH):
        x = x + 0.5 * swiglu(rms_norm(x, ffn1_norm[l]), ffn1_w1[l], ffn1_w3[l], ffn1_w2[l])

        h = rms_norm(x, mix_norm[l])
        proj = h @ w_in[l]
        gq, gk, gv, gr, ga, aq, ak, av = jnp.split(proj, offsets, axis=-1)

        def heads(t, nh):
            return t.reshape(B, S, nh, -1).transpose(0, 2, 1, 3)

        log_a = jax.nn.log_sigmoid((ga @ gla_w_a2[l] + gla_b_a[l]).astype(jnp.float32)) / GLA_TAU
        o_gla = gla_chunked(heads(gq, GLA_HEADS) * (GLA_DK ** -0.5), heads(gk, GLA_HEADS),
                            heads(gv, GLA_HEADS), heads(log_a, GLA_HEADS))
        o_gla = rms_norm(o_gla, gla_out_norm[l].reshape(GLA_HEADS, 1, GLA_DV))
        o_gla = o_gla.transpose(0, 2, 1, 3).reshape(B, S, GLA_WIDTH).astype(x.dtype) * jax.nn.silu(gr)

        q = rope_partial(aq.reshape(B, S, ATT_HEADS, ATT_HD), cos, sin) * (ATT_HD ** -0.5)
        k = rope_partial(ak.reshape(B, S, ATT_HEADS, ATT_HD), cos, sin)
        v = av.reshape(B, S, ATT_HEADS, ATT_HD)
        o_att = dilated_attention(q.transpose(0, 2, 1, 3), k.transpose(0, 2, 1, 3), v.transpose(0, 2, 1, 3))
        o_att = o_att.transpose(0, 2, 1, 3).reshape(B, S, ATT_WIDTH)
        o_att = rms_norm(o_att, att_out_norm[l]).astype(x.dtype)

        x = x + jnp.concatenate([o_gla, o_att], axis=-1) @ w_out[l]

        x = x + 0.5 * swiglu(rms_norm(x, ffn2_norm[l]), ffn2_w1[l], ffn2_w3[l], ffn2_w2[l])

    return rms_norm(x, final_norm)
```

```python
import functools

import jax
import jax.numpy as jnp
import numpy as np
from jax import lax
from jax.experimental import pallas as pl
from jax.experimental.pallas import tpu as pltpu

F32 = jnp.float32
BF16 = jnp.bfloat16

D_MODEL = 1024
D_FF = 2816
GLA_WIDTH = 512
GLA_HEADS = 4
GLA_DV = 128
GLA_DK = 64
GLA_KW = GLA_HEADS * GLA_DK
GLA_RANK = 16
GLA_TAU = 16.0
GLA_CHUNK = 64
ATT_WIDTH = 512
ATT_HEADS = 8
ATT_HD = 64
ROT_DIM = 16
ROPE_THETA = 500000.0
DILATED_PATTERNS = ((128, 1), (512, 4), (2048, 16))
ATT_BLOCK = 128
EPS = 1e-6

LANES = 128
VMEM_LIMIT_BYTES = 56 * 1024 * 1024

TOKEN_TILE = 512
FF_CHUNK = D_FF // 2
GLA_TILE = 512
GLA_HALF = 256
ATT_TILE = 512
NEG = -1e30


def _const_spec(shape):
    nd = len(shape)
    return pl.BlockSpec(shape, lambda *_: (0,) * nd, pipeline_mode=pl.Buffered(1))


def _rms(x, g):
    return x * lax.rsqrt(jnp.mean(x * x, axis=-1, keepdims=True) + EPS) * g


def _swiglu_half_step(h, w1_ref, w3_ref, w2_ref):
    acc = None
    for c in range(D_FF // FF_CHUNK):
        sl = slice(c * FF_CHUNK, (c + 1) * FF_CHUNK)
        a = jnp.dot(h, w1_ref[:, sl], preferred_element_type=F32)
        b = jnp.dot(h, w3_ref[:, sl], preferred_element_type=F32)
        g = (a / (1.0 + jnp.exp(-a)) * b).astype(BF16)
        part = jnp.dot(g, w2_ref[sl, :], preferred_element_type=F32)
        acc = part if acc is None else acc + part
    return acc


def _ffn_proj_kernel(x_ref, pos_ref, g1_ref, w1_ref, w3_ref, w2_ref, gmix_ref, wgla_ref, wga_ref,
                     wa2_ref, ba_ref, watt_ref, freq_ref, sgn1_ref, sgn2_ref,
                     x1_ref, gq_ref, gk_ref, gv_ref, gr_ref, la_ref, aq_ref, ak_ref, av_ref):
    x = x_ref[...]
    h = _rms(x, g1_ref[...]).astype(BF16)
    x1 = x + 0.5 * _swiglu_half_step(h, w1_ref, w3_ref, w2_ref)
    x1_ref[...] = x1

    h2 = _rms(x1, gmix_ref[...]).astype(BF16)
    pg = jnp.dot(h2, wgla_ref[...], preferred_element_type=F32)
    gq_ref[...] = (pg[:, :GLA_KW] * (GLA_DK ** -0.5)).astype(BF16)
    gk_ref[...] = pg[:, GLA_KW:2 * GLA_KW].astype(BF16)
    gv_ref[...] = pg[:, 2 * GLA_KW:2 * GLA_KW + GLA_WIDTH].astype(BF16)
    gr_ref[...] = pg[:, 2 * GLA_KW + GLA_WIDTH:].astype(BF16)

    ga = jnp.dot(h2, wga_ref[...], preferred_element_type=F32).astype(BF16)
    z = jnp.dot(ga, wa2_ref[...], preferred_element_type=F32) + ba_ref[...]
    la_ref[...] = (jnp.minimum(z, 0.0) - jnp.log(1.0 + jnp.exp(-jnp.abs(z)))) * (1.0 / GLA_TAU)

    pa = jnp.dot(h2, watt_ref[...], preferred_element_type=F32)
    av_ref[...] = pa[:, 2 * ATT_WIDTH:].astype(BF16)

    ang = pos_ref[...] * freq_ref[...]
    cos = jnp.cos(ang)
    sin = jnp.sin(ang)
    s1 = sin * sgn1_ref[...]
    s2 = sin * sgn2_ref[...]
    half = ROT_DIM // 2
    for off, ref, scale in ((0, aq_ref, ATT_HD ** -0.5), (ATT_WIDTH, ak_ref, 1.0)):
        for cg in range(ATT_WIDTH // LANES):
            t = pa[:, off + cg * LANES: off + (cg + 1) * LANES]
            r = t * cos + pltpu.roll(t, LANES - half, 1) * s1 + pltpu.roll(t, half, 1) * s2
            ref[:, cg * LANES:(cg + 1) * LANES] = (r * scale).astype(BF16)


def _ffn_proj(x2d, pos, g1, w1, w3, w2, gmix, wgla, wga, wa2, ba, watt, freq, sgn1, sgn2):
    T = x2d.shape[0]
    tm = TOKEN_TILE
    row = lambda n: pl.BlockSpec((tm, n), lambda i: (i, 0))
    in_specs = [row(D_MODEL), row(1), _const_spec(g1.shape), _const_spec(w1.shape), _const_spec(w3.shape),
                _const_spec(w2.shape), _const_spec(gmix.shape), _const_spec(wgla.shape),
                _const_spec(wga.shape), _const_spec(wa2.shape), _const_spec(ba.shape),
                _const_spec(watt.shape), _const_spec(freq.shape), _const_spec(sgn1.shape),
                _const_spec(sgn2.shape)]
    out_shape = [jax.ShapeDtypeStruct((T, D_MODEL), F32),
                 jax.ShapeDtypeStruct((T, GLA_KW), BF16), jax.ShapeDtypeStruct((T, GLA_KW), BF16),
                 jax.ShapeDtypeStruct((T, GLA_WIDTH), BF16), jax.ShapeDtypeStruct((T, GLA_WIDTH), BF16),
                 jax.ShapeDtypeStruct((T, GLA_KW), F32),
                 jax.ShapeDtypeStruct((T, ATT_WIDTH), BF16), jax.ShapeDtypeStruct((T, ATT_WIDTH), BF16),
                 jax.ShapeDtypeStruct((T, ATT_WIDTH), BF16)]
    out_specs = [row(s.shape[1]) for s in out_shape]
    return pl.pallas_call(
        _ffn_proj_kernel, grid=(T // tm,), in_specs=in_specs, out_specs=out_specs, out_shape=out_shape,
        compiler_params=pltpu.CompilerParams(dimension_semantics=("arbitrary",),
                                             vmem_limit_bytes=VMEM_LIMIT_BYTES),
        name="ffn_proj")(x2d, pos, g1, w1, w3, w2, gmix, wgla, wga, wa2, ba, watt, freq, sgn1, sgn2)


def _gla_kernel(q_ref, k_ref, v_ref, la_ref, r_ref, gn_ref, ltri_ref, o_ref, s_ref):
    @pl.when(pl.program_id(1) == 0)
    def _():
        s_ref[...] = jnp.zeros_like(s_ref)

    C = GLA_CHUNK
    r_kk = lax.broadcasted_iota(jnp.int32, (GLA_KW, GLA_KW), 0) // C
    c_kk = lax.broadcasted_iota(jnp.int32, (GLA_KW, GLA_KW), 1) // C
    bd_k = r_kk == c_kk
    r_kv = lax.broadcasted_iota(jnp.int32, (GLA_KW, GLA_WIDTH), 0) // C
    c_kv = lax.broadcasted_iota(jnp.int32, (GLA_KW, GLA_WIDTH), 1) // GLA_DV
    bd_v = r_kv == c_kv
    causal = (lax.broadcasted_iota(jnp.int32, (C, GLA_KW), 1) % C
              <= lax.broadcasted_iota(jnp.int32, (C, GLA_KW), 0))
    ltri = ltri_ref[...]

    for hf in range(GLA_TILE // GLA_HALF):
        base = hf * GLA_HALF
        g = la_ref[base:base + GLA_HALF, :]
        g_hi = g.astype(BF16)
        g_lo = (g - g_hi.astype(F32)).astype(BF16)
        b = (jnp.dot(ltri, g_hi, preferred_element_type=F32)
             + jnp.dot(ltri, g_lo, preferred_element_type=F32))
        for c in range(GLA_HALF // C):
            lo = base + c * C
            bc = b[c * C:(c + 1) * C, :]
            bl = bc[C - 1:C, :]
            qc = q_ref[lo:lo + C, :].astype(F32)
            kc = k_ref[lo:lo + C, :].astype(F32)
            vc = v_ref[lo:lo + C, :]
            qd = (qc * jnp.exp(bc)).astype(BF16)
            ki = (kc * jnp.exp(-bc)).astype(BF16)
            kt = kc * jnp.exp(bl - bc)
            kbd = jnp.where(bd_k, jnp.tile(ki, (GLA_HEADS, 1)), jnp.zeros((), BF16))
            a = lax.dot_general(qd, kbd, (((1,), (1,)), ((), ())), preferred_element_type=F32)
            a = jnp.where(causal, a, 0.0).astype(BF16)
            vbd = jnp.where(bd_v, jnp.tile(vc, (GLA_HEADS, 1)), jnp.zeros((), BF16))
            s_prev = s_ref[...]
            sbd = jnp.where(bd_v, jnp.tile(s_prev.astype(BF16), (1, GLA_HEADS)), jnp.zeros((), BF16))
            o = (jnp.dot(a, vbd, preferred_element_type=F32)
                 + jnp.dot(qd, sbd, preferred_element_type=F32))

            kt_t = kt.T.astype(BF16)
            dec = jnp.exp(bc.T[:, C - 1:C])
            u = jnp.concatenate(
                [jnp.dot(kt_t[h * GLA_DK:(h + 1) * GLA_DK, :], vc[:, h * GLA_DV:(h + 1) * GLA_DV],
                         preferred_element_type=F32) for h in range(GLA_HEADS)], axis=0)
            s_ref[...] = dec * s_prev + u

            gate = r_ref[lo:lo + C, :].astype(F32)
            gate = gate / (1.0 + jnp.exp(-gate))
            for h in range(GLA_HEADS):
                hs = slice(h * GLA_DV, (h + 1) * GLA_DV)
                oh = _rms(o[:, hs], gn_ref[:, hs])
                o_ref[lo:lo + C, hs] = (oh * gate[:, hs]).astype(BF16)


def _gla(gq, gk, gv, la, gr, gn, ltri, batch):
    T = gq.shape[0]
    tiles = T // batch // GLA_TILE
    row = lambda n: pl.BlockSpec((GLA_TILE, n), lambda b, t: (b * tiles + t, 0))
    return pl.pallas_call(
        _gla_kernel, grid=(batch, tiles),
        in_specs=[row(GLA_KW), row(GLA_KW), row(GLA_WIDTH), row(GLA_KW), row(GLA_WIDTH),
                  _const_spec(gn.shape), _const_spec(ltri.shape)],
        out_specs=row(GLA_WIDTH),
        out_shape=jax.ShapeDtypeStruct((T, GLA_WIDTH), BF16),
        scratch_shapes=[pltpu.VMEM((GLA_KW, GLA_DV), F32)],
        compiler_params=pltpu.CompilerParams(dimension_semantics=("arbitrary", "arbitrary"),
                                             vmem_limit_bytes=VMEM_LIMIT_BYTES),
        name="gla")(gq, gk, gv, la, gr, gn, ltri)


def _attn_kernel(*refs, first, last):
    it = iter(refs)
    q_ref, kh_ref, kc_ref, vh_ref, vc_ref = (next(it) for _ in range(5))
    op_ref = lp_ref = gn_ref = ls_ref = None
    if not first:
        op_ref, lp_ref = next(it), next(it)
    if last:
        gn_ref = next(it)
    o_ref = next(it)
    if not last:
        ls_ref = next(it)

    B = ATT_BLOCK
    nq = ATT_TILE // B
    qi = lax.broadcasted_iota(jnp.int32, (B, 2 * B), 0)
    ki = lax.broadcasted_iota(jnp.int32, (B, 2 * B), 1)
    band = (ki >= qi) & (ki <= qi + B)
    lane = lax.broadcasted_iota(jnp.int32, (B, LANES), 1)
    lo_half = lane < ATT_HD
    zero_bf = jnp.zeros((), BF16)

    for qb in range(nq):
        rows = slice(qb * B, (qb + 1) * B)
        if qb == 0:
            kk = jnp.concatenate([kh_ref[...], kc_ref[0:B, :]], axis=0)
            vv = jnp.concatenate([vh_ref[...], vc_ref[0:B, :]], axis=0)
            has_prev = pl.program_id(2) > 0
            mask = band & ((ki >= B) | has_prev)
        else:
            kk = kc_ref[(qb - 1) * B:(qb + 1) * B, :]
            vv = vc_ref[(qb - 1) * B:(qb + 1) * B, :]
            mask = band
        lse_tile = jnp.zeros((B, LANES), F32)
        outs = []
        for hp in range(ATT_WIDTH // LANES):
            cols = slice(hp * LANES, (hp + 1) * LANES)
            qp = q_ref[rows, cols]
            kp = kk[:, cols]
            vp = vv[:, cols]
            o_heads, lse_heads = [], []
            for hh in range(2):
                sel = lo_half if hh == 0 else ~lo_half
                qm = jnp.where(sel, qp, zero_bf)
                s = lax.dot_general(qm, kp, (((1,), (1,)), ((), ())), preferred_element_type=F32)
                s = jnp.where(mask, s, NEG)
                m = jnp.max(s, axis=-1, keepdims=True)
                p = jnp.exp(s - m)
                den = jnp.sum(p, axis=-1, keepdims=True)
                pv = jnp.dot(p.astype(BF16), vp, preferred_element_type=F32)
                o_heads.append(pv / den)
                lse_heads.append(m + jnp.log(den))
            o_pair = jnp.where(lo_half, o_heads[0], o_heads[1])
            lse_pair = jnp.where(lo_half, lse_heads[0], lse_heads[1])
            if not first:
                lp_tile = lp_ref[rows, :]
                lp = jnp.where(lo_half, lp_tile[:, 2 * hp:2 * hp + 1], lp_tile[:, 2 * hp + 1:2 * hp + 2])
                op = op_ref[rows, cols].astype(F32)
                mx = jnp.maximum(lp, lse_pair)
                wa = jnp.exp(lp - mx)
                wc = jnp.exp(lse_pair - mx)
                tot = wa + wc
                o_pair = (wa * op + wc * o_pair) / tot
                lse_pair = mx + jnp.log(tot)
            if last:
                outs.append(o_pair)
            else:
                o_ref[rows, cols] = o_pair.astype(BF16)
                lse_tile = jnp.where(lane == 2 * hp, lse_pair[:, 0:1], lse_tile)
                lse_tile = jnp.where(lane == 2 * hp + 1, lse_pair[:, ATT_HD:ATT_HD + 1], lse_tile)
        if last:
            o_all = jnp.concatenate(outs, axis=1)
            o_ref[rows, :] = _rms(o_all, gn_ref[...]).astype(BF16)
        else:
            ls_ref[rows, :] = lse_tile


def _attn_stage(q, k, v, prev, gn, batch, seq, dilation, first, last):
    T = q.shape[0]
    L = seq // dilation
    W = ATT_WIDTH
    nb = L // ATT_TILE
    hb = ATT_TILE // ATT_BLOCK
    view = lambda a, w: a.reshape(batch, L, dilation * w)
    cur = lambda w: pl.BlockSpec((None, ATT_TILE, w), lambda b, r, j: (b, j, r))
    halo = pl.BlockSpec((None, ATT_BLOCK, W), lambda b, r, j: (b, jnp.maximum(j * hb - 1, 0), r))
    args = [view(q, W), view(k, W), view(k, W), view(v, W), view(v, W)]
    in_specs = [cur(W), halo, cur(W), halo, cur(W)]
    if not first:
        args += [view(prev[0], W), view(prev[1], LANES)]
        in_specs += [cur(W), cur(LANES)]
    if last:
        args.append(gn)
        in_specs.append(_const_spec(gn.shape))
        out_shape = jax.ShapeDtypeStruct((batch, L, dilation * W), BF16)
        out_specs = cur(W)
    else:
        out_shape = [jax.ShapeDtypeStruct((batch, L, dilation * W), BF16),
                     jax.ShapeDtypeStruct((batch, L, dilation * LANES), F32)]
        out_specs = [cur(W), cur(LANES)]
    res = pl.pallas_call(
        functools.partial(_attn_kernel, first=first, last=last),
        grid=(batch, dilation, nb), in_specs=in_specs, out_specs=out_specs, out_shape=out_shape,
        compiler_params=pltpu.CompilerParams(
            dimension_semantics=("arbitrary", "arbitrary", "arbitrary"),
            vmem_limit_bytes=VMEM_LIMIT_BYTES),
        name=f"attn_d{dilation}")(*args)
    if last:
        return res.reshape(T, W)
    return res[0].reshape(T, W), res[1].reshape(T, LANES)


def _out_ffn_kernel(x1_ref, og_ref, oa_ref, wog_ref, woa_ref, g2_ref, w1_ref, w3_ref, w2_ref, gf_ref,
                    out_ref):
    x2 = (x1_ref[...]
          + jnp.dot(og_ref[...], wog_ref[...], preferred_element_type=F32)
          + jnp.dot(oa_ref[...], woa_ref[...], preferred_element_type=F32))
    h = _rms(x2, g2_ref[...]).astype(BF16)
    x3 = x2 + 0.5 * _swiglu_half_step(h, w1_ref, w3_ref, w2_ref)
    out_ref[...] = _rms(x3, gf_ref[...])


def _out_ffn(x1, og, oa, wog, woa, g2, w1, w3, w2, gf):
    T = x1.shape[0]
    tm = TOKEN_TILE
    row = lambda n: pl.BlockSpec((tm, n), lambda i: (i, 0))
    return pl.pallas_call(
        _out_ffn_kernel, grid=(T // tm,),
        in_specs=[row(D_MODEL), row(GLA_WIDTH), row(ATT_WIDTH), _const_spec(wog.shape),
                  _const_spec(woa.shape), _const_spec(g2.shape), _const_spec(w1.shape),
                  _const_spec(w3.shape), _const_spec(w2.shape), _const_spec(gf.shape)],
        out_specs=row(D_MODEL),
        out_shape=jax.ShapeDtypeStruct((T, D_MODEL), F32),
        compiler_params=pltpu.CompilerParams(dimension_semantics=("arbitrary",),
                                             vmem_limit_bytes=VMEM_LIMIT_BYTES),
        name="out_ffn")(x1, og, oa, wog, woa, g2, w1, w3, w2, gf)


def _rope_tables():
    j = np.arange(LANES) % ATT_HD
    half = ROT_DIM // 2
    inv_freq = ROPE_THETA ** (-np.arange(0, ROT_DIM, 2, dtype=np.float32) / ROT_DIM)
    freq = np.where(j < ROT_DIM, inv_freq[j % half], 0.0).astype(np.float32)
    sgn1 = np.where(j < half, -1.0, 0.0).astype(np.float32)
    sgn2 = np.where((j >= half) & (j < ROT_DIM), 1.0, 0.0).astype(np.float32)
    return freq[None, :], sgn1[None, :], sgn2[None, :]


def _chunk_tril():
    i = np.arange(GLA_HALF)
    same = (i[:, None] // GLA_CHUNK) == (i[None, :] // GLA_CHUNK)
    return jnp.asarray((same & (i[None, :] <= i[:, None])).astype(np.float32), dtype=BF16)


def kernel(x, positions, ffn1_norm, ffn1_w1, ffn1_w3, ffn1_w2, mix_norm, w_in, gla_w_a2, gla_b_a,
           gla_out_norm, att_out_norm, w_out, ffn2_norm, ffn2_w1, ffn2_w3, ffn2_w2, final_norm):
    batch, seq, _ = x.shape
    T = batch * seq
    depth = ffn1_norm.shape[0]
    assert depth == 1, "the final norm is fused into the single layer's last kernel"
    freq, sgn1, sgn2 = (jnp.asarray(t) for t in _rope_tables())
    ltri = _chunk_tril()
    pos = positions.astype(F32).reshape(T, 1)
    xs = x.reshape(T, D_MODEL)

    o_gla_end = 2 * GLA_KW + 2 * GLA_WIDTH
    o_ga_end = o_gla_end + GLA_RANK
    for l in range(depth):
        wi = w_in[l].astype(BF16)
        wgla = wi[:, :o_gla_end]
        wga = jnp.pad(wi[:, o_gla_end:o_ga_end], ((0, 0), (0, LANES - GLA_RANK)))
        watt = wi[:, o_ga_end:]
        wa2 = jnp.pad(gla_w_a2[l].astype(BF16), ((0, LANES - GLA_RANK), (0, 0)))
        x1, gq, gk, gv, gr, la, aq, ak, av = _ffn_proj(
            xs, pos, ffn1_norm[l][None, :], ffn1_w1[l].astype(BF16), ffn1_w3[l].astype(BF16),
            ffn1_w2[l].astype(BF16), mix_norm[l][None, :], wgla, wga, wa2, gla_b_a[l][None, :], watt,
            freq, sgn1, sgn2)

        o_gla = _gla(gq, gk, gv, la, gr, gla_out_norm[l][None, :], ltri, batch)

        prev = None
        n_pat = len(DILATED_PATTERNS)
        for idx, (window, dilation) in enumerate(DILATED_PATTERNS):
            assert window // dilation == ATT_BLOCK
            prev = _attn_stage(aq, ak, av, prev, att_out_norm[l][None, :], batch, seq, dilation,
                               first=idx == 0, last=idx == n_pat - 1)
        o_att = prev

        wo = w_out[l].astype(BF16)
        xs = _out_ffn(x1, o_gla, o_att, wo[:GLA_WIDTH], wo[GLA_WIDTH:], ffn2_norm[l][None, :],
                      ffn2_w1[l].astype(BF16), ffn2_w3[l].astype(BF16), ffn2_w2[l].astype(BF16),
                      final_norm[None, :])
    return xs.reshape(batch, seq, D_MODEL)
```

```python
import functools
import math

import jax
import jax.numpy as jnp
import numpy as np
from jax import lax
from jax.experimental import pallas as pl
from jax.experimental.pallas import tpu as pltpu

F32 = jnp.float32
BF16 = jnp.bfloat16

D_MODEL = 1024
D_FF = 2816
GLA_WIDTH = 512
GLA_HEADS = 4
GLA_DV = 128
GLA_DK = 64
GLA_KW = GLA_HEADS * GLA_DK
GLA_RANK = 16
GLA_TAU = 16.0
GLA_CHUNK = 64
ATT_WIDTH = 512
ATT_HEADS = 8
ATT_HD = 64
ROT_DIM = 16
ROPE_THETA = 500000.0
DILATED_PATTERNS = ((128, 1), (512, 4), (2048, 16))
ATT_BLOCK = 128
EPS = 1e-6

LANES = 128
VMEM_LIMIT_BYTES = 56 * 1024 * 1024

TOKEN_TILE = 512
FF_CHUNK = D_FF // 2
GLA_TILE = 512
GLA_HALF = 256
ATT_DILATIONS = tuple(sorted(d for _, d in DILATED_PATTERNS))
ATT_SUPER = max(ATT_DILATIONS) * ATT_BLOCK
ATT_COMBINE_ROWS = 256
NEG = -1e30
LOG2E = math.log2(math.e)

assert all(w // d == ATT_BLOCK for w, d in DILATED_PATTERNS)
assert ATT_DILATIONS[0] == 1 and all(ATT_SUPER % (d * ATT_BLOCK) == 0 for d in ATT_DILATIONS)


def _const_spec(shape):
    nd = len(shape)
    return pl.BlockSpec(shape, lambda *_: (0,) * nd, pipeline_mode=pl.Buffered(1))


def _rms(x, g):
    return x * lax.rsqrt(jnp.mean(x * x, axis=-1, keepdims=True) + EPS) * g


def _swiglu_half_step(h, w1_ref, w3_ref, w2_ref):
    acc = None
    for c in range(D_FF // FF_CHUNK):
        sl = slice(c * FF_CHUNK, (c + 1) * FF_CHUNK)
        a = jnp.dot(h, w1_ref[:, sl], preferred_element_type=F32)
        b = jnp.dot(h, w3_ref[:, sl], preferred_element_type=F32)
        g = (a / (1.0 + jnp.exp(-a)) * b).astype(BF16)
        part = jnp.dot(g, w2_ref[sl, :], preferred_element_type=F32)
        acc = part if acc is None else acc + part
    return acc


def _ffn_proj_kernel(x_ref, pos_ref, g1_ref, w1_ref, w3_ref, w2_ref, gmix_ref, wgla_ref, wga_ref,
                     wa2_ref, ba_ref, watt_ref, freq_ref, sgn1_ref, sgn2_ref,
                     x1_ref, gq_ref, gk_ref, gv_ref, gr_ref, la_ref, aq_ref, ak_ref, av_ref):
    x = x_ref[...]
    h = _rms(x, g1_ref[...]).astype(BF16)
    x1 = x + 0.5 * _swiglu_half_step(h, w1_ref, w3_ref, w2_ref)
    x1_ref[...] = x1

    h2 = _rms(x1, gmix_ref[...]).astype(BF16)
    pg = jnp.dot(h2, wgla_ref[...], preferred_element_type=F32)
    gq_ref[...] = (pg[:, :GLA_KW] * (GLA_DK ** -0.5)).astype(BF16)
    gk_ref[...] = pg[:, GLA_KW:2 * GLA_KW].astype(BF16)
    gv_ref[...] = pg[:, 2 * GLA_KW:2 * GLA_KW + GLA_WIDTH].astype(BF16)
    gr_ref[...] = pg[:, 2 * GLA_KW + GLA_WIDTH:].astype(BF16)

    ga = jnp.dot(h2, wga_ref[...], preferred_element_type=F32).astype(BF16)
    z = jnp.dot(ga, wa2_ref[...], preferred_element_type=F32) + ba_ref[...]
    la_ref[...] = (jnp.minimum(z, 0.0) - jnp.log(1.0 + jnp.exp(-jnp.abs(z)))) * (1.0 / GLA_TAU)

    pa = jnp.dot(h2, watt_ref[...], preferred_element_type=F32)
    av_ref[...] = pa[:, 2 * ATT_WIDTH:].astype(BF16)

    ang = pos_ref[...] * freq_ref[...]
    cos = jnp.cos(ang)
    sin = jnp.sin(ang)
    s1 = sin * sgn1_ref[...]
    s2 = sin * sgn2_ref[...]
    half = ROT_DIM // 2
    for off, ref, scale in ((0, aq_ref, ATT_HD ** -0.5 * LOG2E), (ATT_WIDTH, ak_ref, 1.0)):
        for cg in range(ATT_WIDTH // LANES):
            t = pa[:, off + cg * LANES: off + (cg + 1) * LANES]
            r = t * cos + pltpu.roll(t, LANES - half, 1) * s1 + pltpu.roll(t, half, 1) * s2
            ref[:, cg * LANES:(cg + 1) * LANES] = (r * scale).astype(BF16)


def _ffn_proj(x2d, pos, g1, w1, w3, w2, gmix, wgla, wga, wa2, ba, watt, freq, sgn1, sgn2):
    T = x2d.shape[0]
    tm = TOKEN_TILE
    row = lambda n: pl.BlockSpec((tm, n), lambda i: (i, 0))
    in_specs = [row(D_MODEL), row(1), _const_spec(g1.shape), _const_spec(w1.shape), _const_spec(w3.shape),
                _const_spec(w2.shape), _const_spec(gmix.shape), _const_spec(wgla.shape),
                _const_spec(wga.shape), _const_spec(wa2.shape), _const_spec(ba.shape),
                _const_spec(watt.shape), _const_spec(freq.shape), _const_spec(sgn1.shape),
                _const_spec(sgn2.shape)]
    out_shape = [jax.ShapeDtypeStruct((T, D_MODEL), F32),
                 jax.ShapeDtypeStruct((T, GLA_KW), BF16), jax.ShapeDtypeStruct((T, GLA_KW), BF16),
                 jax.ShapeDtypeStruct((T, GLA_WIDTH), BF16), jax.ShapeDtypeStruct((T, GLA_WIDTH), BF16),
                 jax.ShapeDtypeStruct((T, GLA_KW), F32),
                 jax.ShapeDtypeStruct((T, ATT_WIDTH), BF16), jax.ShapeDtypeStruct((T, ATT_WIDTH), BF16),
                 jax.ShapeDtypeStruct((T, ATT_WIDTH), BF16)]
    out_specs = [row(s.shape[1]) for s in out_shape]
    return pl.pallas_call(
        _ffn_proj_kernel, grid=(T // tm,), in_specs=in_specs, out_specs=out_specs, out_shape=out_shape,
        compiler_params=pltpu.CompilerParams(dimension_semantics=("arbitrary",),
                                             vmem_limit_bytes=VMEM_LIMIT_BYTES),
        name="ffn_proj")(x2d, pos, g1, w1, w3, w2, gmix, wgla, wga, wa2, ba, watt, freq, sgn1, sgn2)


def _gla_kernel(q_ref, k_ref, v_ref, la_ref, r_ref, gn_ref, ltri_ref, o_ref, s_ref):
    @pl.when(pl.program_id(1) == 0)
    def _():
        s_ref[...] = jnp.zeros_like(s_ref)

    C = GLA_CHUNK
    r_kk = lax.broadcasted_iota(jnp.int32, (GLA_KW, GLA_KW), 0) // C
    c_kk = lax.broadcasted_iota(jnp.int32, (GLA_KW, GLA_KW), 1) // C
    bd_k = r_kk == c_kk
    r_kv = lax.broadcasted_iota(jnp.int32, (GLA_KW, GLA_WIDTH), 0) // C
    c_kv = lax.broadcasted_iota(jnp.int32, (GLA_KW, GLA_WIDTH), 1) // GLA_DV
    bd_v = r_kv == c_kv
    causal = (lax.broadcasted_iota(jnp.int32, (C, GLA_KW), 1) % C
              <= lax.broadcasted_iota(jnp.int32, (C, GLA_KW), 0))
    ltri = ltri_ref[...]

    for hf in range(GLA_TILE // GLA_HALF):
        base = hf * GLA_HALF
        g = la_ref[base:base + GLA_HALF, :]
        g_hi = g.astype(BF16)
        g_lo = (g - g_hi.astype(F32)).astype(BF16)
        b = (jnp.dot(ltri, g_hi, preferred_element_type=F32)
             + jnp.dot(ltri, g_lo, preferred_element_type=F32))
        for c in range(GLA_HALF // C):
            lo = base + c * C
            bc = b[c * C:(c + 1) * C, :]
            bl = bc[C - 1:C, :]
            qc = q_ref[lo:lo + C, :].astype(F32)
            kc = k_ref[lo:lo + C, :].astype(F32)
            vc = v_ref[lo:lo + C, :]
            qd = (qc * jnp.exp(bc)).astype(BF16)
            ki = (kc * jnp.exp(-bc)).astype(BF16)
            kt = kc * jnp.exp(bl - bc)
            kbd = jnp.where(bd_k, jnp.tile(ki, (GLA_HEADS, 1)), jnp.zeros((), BF16))
            a = lax.dot_general(qd, kbd, (((1,), (1,)), ((), ())), preferred_element_type=F32)
            a = jnp.where(causal, a, 0.0).astype(BF16)
            vbd = jnp.where(bd_v, jnp.tile(vc, (GLA_HEADS, 1)), jnp.zeros((), BF16))
            s_prev = s_ref[...]
            sbd = jnp.where(bd_v, jnp.tile(s_prev.astype(BF16), (1, GLA_HEADS)), jnp.zeros((), BF16))
            o = (jnp.dot(a, vbd, preferred_element_type=F32)
                 + jnp.dot(qd, sbd, preferred_element_type=F32))

            kt_t = kt.T.astype(BF16)
            dec = jnp.exp(bc.T[:, C - 1:C])
            u = jnp.concatenate(
                [jnp.dot(kt_t[h * GLA_DK:(h + 1) * GLA_DK, :], vc[:, h * GLA_DV:(h + 1) * GLA_DV],
                         preferred_element_type=F32) for h in range(GLA_HEADS)], axis=0)
            s_ref[...] = dec * s_prev + u

            gate = r_ref[lo:lo + C, :].astype(F32)
            gate = gate / (1.0 + jnp.exp(-gate))
            for h in range(GLA_HEADS):
                hs = slice(h * GLA_DV, (h + 1) * GLA_DV)
                oh = _rms(o[:, hs], gn_ref[:, hs])
                o_ref[lo:lo + C, hs] = (oh * gate[:, hs]).astype(BF16)


def _gla(gq, gk, gv, la, gr, gn, ltri, batch):
    T = gq.shape[0]
    tiles = T // batch // GLA_TILE
    row = lambda n: pl.BlockSpec((GLA_TILE, n), lambda b, t: (b * tiles + t, 0))
    return pl.pallas_call(
        _gla_kernel, grid=(batch, tiles),
        in_specs=[row(GLA_KW), row(GLA_KW), row(GLA_WIDTH), row(GLA_KW), row(GLA_WIDTH),
                  _const_spec(gn.shape), _const_spec(ltri.shape)],
        out_specs=row(GLA_WIDTH),
        out_shape=jax.ShapeDtypeStruct((T, GLA_WIDTH), BF16),
        scratch_shapes=[pltpu.VMEM((GLA_KW, GLA_DV), F32)],
        compiler_params=pltpu.CompilerParams(dimension_semantics=("arbitrary", "arbitrary"),
                                             vmem_limit_bytes=VMEM_LIMIT_BYTES),
        name="gla")(gq, gk, gv, la, gr, gn, ltri)


def _attn_kernel(q_ref, k_ref, v_ref, o_ref, xf_ref, *scratch):
    B = ATT_BLOCK
    T = ATT_SUPER
    nd = len(ATT_DILATIONS)
    qd = dict(zip(ATT_DILATIONS[1:], scratch[:nd - 1]))
    kc = dict(zip(ATT_DILATIONS, scratch[nd - 1:2 * nd - 1]))
    vc = dict(zip(ATT_DILATIONS, scratch[2 * nd - 1:3 * nd - 1]))
    acc = scratch[3 * nd - 1:]
    acc_o = dict(zip(ATT_DILATIONS, acc[0::3]))
    acc_d = dict(zip(ATT_DILATIONS, acc[1::3]))
    acc_m = dict(zip(ATT_DILATIONS, acc[2::3]))
    tile = pl.program_id(2)

    def stage(src_ref, dst, halo):
        xf_ref[...] = src_ref[...].astype(F32)
        for d in ATT_DILATIONS[1:]:
            n = T // d
            for r in range(d):
                lo_row = r * (n + halo) + halo
                dst[d][lo_row:lo_row + n, :] = xf_ref[pl.ds(r, n, stride=d), :].astype(BF16)

    stage(q_ref, qd, 0)
    stage(k_ref, kc, B)
    stage(v_ref, vc, B)
    kc[1][B:, :] = k_ref[...]
    vc[1][B:, :] = v_ref[...]

    @pl.when(tile == 0)
    def _():
        zeros = jnp.zeros((B, LANES), BF16)
        for d in ATT_DILATIONS:
            for r in range(d):
                row = r * (T // d + B)
                kc[d][row:row + B, :] = zeros
                vc[d][row:row + B, :] = zeros

    qi = lax.broadcasted_iota(jnp.int32, (B, 2 * B), 0)
    ki = lax.broadcasted_iota(jnp.int32, (B, 2 * B), 1)
    band = (ki >= qi) & (ki <= qi + B)
    bias_band = jnp.where(band, 0.0, NEG).astype(F32)
    bias_first = jnp.where(band & ((ki >= B) | (tile > 0)), 0.0, NEG).astype(F32)
    lane = lax.broadcasted_iota(jnp.int32, (B, LANES), 1)
    lo = lane < ATT_HD
    zero_bf = jnp.zeros((), BF16)
    ones_v = jnp.ones((2 * B, LANES), BF16)

    def block(q_src, q_row, d, k_row, bias, rows_out):
        qp = q_src[q_row:q_row + B, :]
        kk = kc[d][k_row:k_row + 2 * B, :]
        vv = vc[d][k_row:k_row + 2 * B, :]
        qs = jnp.concatenate([jnp.where(lo, qp, zero_bf), jnp.where(lo, zero_bf, qp)], axis=0)
        s = lax.dot_general(qs, kk, (((1,), (1,)), ((), ())), preferred_element_type=F32)
        s = s + jnp.concatenate([bias, bias], axis=0)
        m = jnp.max(s, axis=-1, keepdims=True)
        p = jnp.exp2(s - m).astype(BF16)
        pv = jnp.dot(p, jnp.concatenate([vv, ones_v], axis=1), preferred_element_type=F32)
        acc_o[d][rows_out, :] = jnp.where(lo, pv[:B, :LANES], pv[B:, :LANES])
        acc_d[d][rows_out, :] = jnp.where(lo, pv[:B, LANES:], pv[B:, LANES:])
        acc_m[d][rows_out, :] = jnp.where(lo, m[:B], m[B:])

    for d in ATT_DILATIONS:
        n = T // d
        for r in range(d):
            for jb in range(n // B):
                q_src, q_row = (q_ref, jb * B) if d == 1 else (qd[d], r * n + jb * B)
                rows_out = pl.ds(r + d * B * jb, B, stride=d) if d > 1 else pl.ds(jb * B, B)
                block(q_src, q_row, d, r * (n + B) + jb * B, bias_first if jb == 0 else bias_band,
                      rows_out)

    for c in range(T // ATT_COMBINE_ROWS):
        rows = slice(c * ATT_COMBINE_ROWS, (c + 1) * ATT_COMBINE_ROWS)
        ms = [acc_m[d][rows, :] for d in ATT_DILATIONS]
        mx = functools.reduce(jnp.maximum, ms)
        es = [jnp.exp2(m - mx) for m in ms]
        num = sum(e * acc_o[d][rows, :] for e, d in zip(es, ATT_DILATIONS))
        den = sum(e * acc_d[d][rows, :] for e, d in zip(es, ATT_DILATIONS))
        o_ref[rows, :] = (num / den).astype(o_ref.dtype)

    for d in ATT_DILATIONS:
        n = T // d
        for r in range(d):
            row = r * (n + B)
            kc[d][row:row + B, :] = kc[d][row + n:row + n + B, :]
            vc[d][row:row + B, :] = vc[d][row + n:row + n + B, :]


def _attn(q, k, v, batch, seq):
    T = q.shape[0]
    tiles = seq // ATT_SUPER
    spec = pl.BlockSpec((ATT_SUPER, LANES), lambda b, h, t: (b * tiles + t, h))
    rows = lambda n, dt: pltpu.VMEM((n, LANES), dt)
    kv_rows = [ATT_SUPER + d * ATT_BLOCK for d in ATT_DILATIONS]
    scratch = ([rows(ATT_SUPER, F32)] + [rows(ATT_SUPER, BF16) for _ in ATT_DILATIONS[1:]]
               + [rows(n, BF16) for n in kv_rows] + [rows(n, BF16) for n in kv_rows]
               + [rows(ATT_SUPER, F32) for _ in range(3 * len(ATT_DILATIONS))])
    return pl.pallas_call(
        _attn_kernel, grid=(batch, ATT_WIDTH // LANES, tiles), in_specs=[spec, spec, spec],
        out_specs=spec, out_shape=jax.ShapeDtypeStruct((T, ATT_WIDTH), BF16), scratch_shapes=scratch,
        compiler_params=pltpu.CompilerParams(
            dimension_semantics=("arbitrary", "arbitrary", "arbitrary"),
            vmem_limit_bytes=VMEM_LIMIT_BYTES),
        name="attn")(q, k, v)


def _out_ffn_kernel(x1_ref, og_ref, oa_ref, ga_ref, wog_ref, woa_ref, g2_ref, w1_ref, w3_ref, w2_ref,
                    gf_ref, out_ref):
    oa = _rms(oa_ref[...].astype(F32), ga_ref[...]).astype(BF16)
    x2 = (x1_ref[...]
          + jnp.dot(og_ref[...], wog_ref[...], preferred_element_type=F32)
          + jnp.dot(oa, woa_ref[...], preferred_element_type=F32))
    h = _rms(x2, g2_ref[...]).astype(BF16)
    x3 = x2 + 0.5 * _swiglu_half_step(h, w1_ref, w3_ref, w2_ref)
    out_ref[...] = _rms(x3, gf_ref[...])


def _out_ffn(x1, og, oa, ga, wog, woa, g2, w1, w3, w2, gf):
    T = x1.shape[0]
    tm = TOKEN_TILE
    row = lambda n: pl.BlockSpec((tm, n), lambda i: (i, 0))
    return pl.pallas_call(
        _out_ffn_kernel, grid=(T // tm,),
        in_specs=[row(D_MODEL), row(GLA_WIDTH), row(ATT_WIDTH), _const_spec(ga.shape),
                  _const_spec(wog.shape), _const_spec(woa.shape), _const_spec(g2.shape),
                  _const_spec(w1.shape), _const_spec(w3.shape), _const_spec(w2.shape),
                  _const_spec(gf.shape)],
        out_specs=row(D_MODEL),
        out_shape=jax.ShapeDtypeStruct((T, D_MODEL), F32),
        compiler_params=pltpu.CompilerParams(dimension_semantics=("arbitrary",),
                                             vmem_limit_bytes=VMEM_LIMIT_BYTES),
        name="out_ffn")(x1, og, oa, ga, wog, woa, g2, w1, w3, w2, gf)


def _rope_tables():
    j = np.arange(LANES) % ATT_HD
    half = ROT_DIM // 2
    inv_freq = ROPE_THETA ** (-np.arange(0, ROT_DIM, 2, dtype=np.float32) / ROT_DIM)
    freq = np.where(j < ROT_DIM, inv_freq[j % half], 0.0).astype(np.float32)
    sgn1 = np.where(j < half, -1.0, 0.0).astype(np.float32)
    sgn2 = np.where((j >= half) & (j < ROT_DIM), 1.0, 0.0).astype(np.float32)
    return freq[None, :], sgn1[None, :], sgn2[None, :]


def _chunk_tril():
    i = np.arange(GLA_HALF)
    same = (i[:, None] // GLA_CHUNK) == (i[None, :] // GLA_CHUNK)
    return jnp.asarray((same & (i[None, :] <= i[:, None])).astype(np.float32), dtype=BF16)


def kernel(x, positions, ffn1_norm, ffn1_w1, ffn1_w3, ffn1_w2, mix_norm, w_in, gla_w_a2, gla_b_a,
           gla_out_norm, att_out_norm, w_out, ffn2_norm, ffn2_w1, ffn2_w3, ffn2_w2, final_norm):
    batch, seq, _ = x.shape
    T = batch * seq
    depth = ffn1_norm.shape[0]
    assert depth == 1, "the final norm is fused into the single layer's last kernel"
    freq, sgn1, sgn2 = (jnp.asarray(t) for t in _rope_tables())
    ltri = _chunk_tril()
    pos = positions.astype(F32).reshape(T, 1)
    xs = x.reshape(T, D_MODEL)

    o_gla_end = 2 * GLA_KW + 2 * GLA_WIDTH
    o_ga_end = o_gla_end + GLA_RANK
    for l in range(depth):
        wi = w_in[l].astype(BF16)
        wgla = wi[:, :o_gla_end]
        wga = jnp.pad(wi[:, o_gla_end:o_ga_end], ((0, 0), (0, LANES - GLA_RANK)))
        watt = wi[:, o_ga_end:]
        wa2 = jnp.pad(gla_w_a2[l].astype(BF16), ((0, LANES - GLA_RANK), (0, 0)))
        x1, gq, gk, gv, gr, la, aq, ak, av = _ffn_proj(
            xs, pos, ffn1_norm[l][None, :], ffn1_w1[l].astype(BF16), ffn1_w3[l].astype(BF16),
            ffn1_w2[l].astype(BF16), mix_norm[l][None, :], wgla, wga, wa2, gla_b_a[l][None, :], watt,
            freq, sgn1, sgn2)

        o_gla = _gla(gq, gk, gv, la, gr, gla_out_norm[l][None, :], ltri, batch)

        o_att = _attn(aq, ak, av, batch, seq)

        wo = w_out[l].astype(BF16)
        xs = _out_ffn(x1, o_gla, o_att, att_out_norm[l][None, :], wo[:GLA_WIDTH], wo[GLA_WIDTH:],
                      ffn2_norm[l][None, :], ffn2_w1[l].astype(BF16), ffn2_w3[l].astype(BF16),
                      ffn2_w2[l].astype(BF16), final_norm[None, :])
    return xs.reshape(batch, seq, D_MODEL)
```

```python
import functools
import math

import jax
import jax.numpy as jnp
import numpy as np
from jax import lax
from jax.experimental import pallas as pl
from jax.experimental.pallas import tpu as pltpu

F32 = jnp.float32
BF16 = jnp.bfloat16

D_MODEL = 1024
D_FF = 2816
GLA_WIDTH = 512
GLA_HEADS = 4
GLA_DV = 128
GLA_DK = 64
GLA_KW = GLA_HEADS * GLA_DK
GLA_RANK = 16
GLA_TAU = 16.0
GLA_CHUNK = 64
ATT_WIDTH = 512
ATT_HEADS = 8
ATT_HD = 64
ROT_DIM = 16
ROPE_THETA = 500000.0
DILATED_PATTERNS = ((128, 1), (512, 4), (2048, 16))
ATT_BLOCK = 128
EPS = 1e-6

LANES = 128
VMEM_LIMIT_BYTES = 56 * 1024 * 1024

TOKEN_TILE = 512
MXU_TILE = 256
FF_SPLITS = (0, (D_FF // MXU_TILE // 2) * MXU_TILE, D_FF)
assert D_FF % MXU_TILE == 0
GLA_TILE = 512
GLA_HALF = 256
ATT_DILATIONS = tuple(sorted(d for _, d in DILATED_PATTERNS))
ATT_SUPER = max(ATT_DILATIONS) * ATT_BLOCK
ATT_COMBINE_ROWS = 256
NEG = -1e30
LOG2E = math.log2(math.e)

assert all(w // d == ATT_BLOCK for w, d in DILATED_PATTERNS)
assert ATT_DILATIONS[0] == 1 and all(ATT_SUPER % (d * ATT_BLOCK) == 0 for d in ATT_DILATIONS)


def _const_spec(shape):
    nd = len(shape)
    return pl.BlockSpec(shape, lambda *_: (0,) * nd, pipeline_mode=pl.Buffered(1))


def _rms(x, g):
    return x * lax.rsqrt(jnp.mean(x * x, axis=-1, keepdims=True) + EPS) * g


def _swiglu_half_step(h, w1_ref, w3_ref, w2_ref):
    acc = None
    for lo, hi in zip(FF_SPLITS[:-1], FF_SPLITS[1:]):
        sl = slice(lo, hi)
        a = jnp.dot(h, w1_ref[:, sl], preferred_element_type=F32)
        b = jnp.dot(h, w3_ref[:, sl], preferred_element_type=F32)
        g = (a / (1.0 + jnp.exp(-a)) * b).astype(BF16)
        part = jnp.dot(g, w2_ref[sl, :], preferred_element_type=F32)
        acc = part if acc is None else acc + part
    return acc


def _ffn_proj_kernel(x_ref, pos_ref, g1_ref, w1_ref, w3_ref, w2_ref, gmix_ref, wgla_ref, wga_ref,
                     wa2_ref, ba_ref, watt_ref, freq_ref, sgn1_ref, sgn2_ref,
                     x1_ref, gq_ref, gk_ref, gv_ref, gr_ref, la_ref, aq_ref, ak_ref, av_ref):
    x = x_ref[...]
    h = _rms(x, g1_ref[...]).astype(BF16)
    x1 = x + 0.5 * _swiglu_half_step(h, w1_ref, w3_ref, w2_ref)
    x1_ref[...] = x1

    h2 = _rms(x1, gmix_ref[...]).astype(BF16)
    pa = jnp.dot(h2, watt_ref[...], preferred_element_type=F32)
    av_ref[...] = pa[:, 2 * ATT_WIDTH:].astype(BF16)

    ang = pos_ref[...] * freq_ref[...]
    cos = jnp.cos(ang)
    sin = jnp.sin(ang)
    s1 = sin * sgn1_ref[...]
    s2 = sin * sgn2_ref[...]
    half = ROT_DIM // 2
    for off, ref, scale in ((0, aq_ref, ATT_HD ** -0.5 * LOG2E), (ATT_WIDTH, ak_ref, 1.0)):
        for cg in range(ATT_WIDTH // LANES):
            t = pa[:, off + cg * LANES: off + (cg + 1) * LANES]
            r = t * cos + pltpu.roll(t, LANES - half, 1) * s1 + pltpu.roll(t, half, 1) * s2
            ref[:, cg * LANES:(cg + 1) * LANES] = (r * scale).astype(BF16)

    ga = jnp.dot(h2, wga_ref[...], preferred_element_type=F32).astype(BF16)
    z = jnp.dot(ga, wa2_ref[...], preferred_element_type=F32) + ba_ref[...]
    la_ref[...] = (jnp.minimum(z, 0.0) - jnp.log(1.0 + jnp.exp(-jnp.abs(z)))) * (1.0 / GLA_TAU)

    pg = jnp.dot(h2, wgla_ref[...], preferred_element_type=F32)
    gq_ref[...] = (pg[:, :GLA_KW] * (GLA_DK ** -0.5)).astype(BF16)
    gk_ref[...] = pg[:, GLA_KW:2 * GLA_KW].astype(BF16)
    gv_ref[...] = pg[:, 2 * GLA_KW:2 * GLA_KW + GLA_WIDTH].astype(BF16)
    gr = pg[:, 2 * GLA_KW + GLA_WIDTH:]
    gr_ref[...] = (gr / (1.0 + jnp.exp(-gr))).astype(BF16)


def _ffn_proj(x2d, pos, g1, w1, w3, w2, gmix, wgla, wga, wa2, ba, watt, freq, sgn1, sgn2):
    T = x2d.shape[0]
    tm = TOKEN_TILE
    row = lambda n: pl.BlockSpec((tm, n), lambda i: (i, 0))
    in_specs = [row(D_MODEL), row(1), _const_spec(g1.shape), _const_spec(w1.shape), _const_spec(w3.shape),
                _const_spec(w2.shape), _const_spec(gmix.shape), _const_spec(wgla.shape),
                _const_spec(wga.shape), _const_spec(wa2.shape), _const_spec(ba.shape),
                _const_spec(watt.shape), _const_spec(freq.shape), _const_spec(sgn1.shape),
                _const_spec(sgn2.shape)]
    out_shape = [jax.ShapeDtypeStruct((T, D_MODEL), F32),
                 jax.ShapeDtypeStruct((T, GLA_KW), BF16), jax.ShapeDtypeStruct((T, GLA_KW), BF16),
                 jax.ShapeDtypeStruct((T, GLA_WIDTH), BF16), jax.ShapeDtypeStruct((T, GLA_WIDTH), BF16),
                 jax.ShapeDtypeStruct((T, GLA_KW), F32),
                 jax.ShapeDtypeStruct((T, ATT_WIDTH), BF16), jax.ShapeDtypeStruct((T, ATT_WIDTH), BF16),
                 jax.ShapeDtypeStruct((T, ATT_WIDTH), BF16)]
    out_specs = [row(s.shape[1]) for s in out_shape]
    return pl.pallas_call(
        _ffn_proj_kernel, grid=(T // tm,), in_specs=in_specs, out_specs=out_specs, out_shape=out_shape,
        compiler_params=pltpu.CompilerParams(dimension_semantics=("arbitrary",),
                                             vmem_limit_bytes=VMEM_LIMIT_BYTES),
        name="ffn_proj")(x2d, pos, g1, w1, w3, w2, gmix, wgla, wga, wa2, ba, watt, freq, sgn1, sgn2)


def _gla_kernel(q_ref, k_ref, v_ref, la_ref, r_ref, gn_ref, ltri_ref, o_ref, s_ref):
    @pl.when(pl.program_id(1) == 0)
    def _():
        s_ref[...] = jnp.zeros_like(s_ref)

    C = GLA_CHUNK
    r_kk = lax.broadcasted_iota(jnp.int32, (GLA_KW, GLA_KW), 0) // C
    c_kk = lax.broadcasted_iota(jnp.int32, (GLA_KW, GLA_KW), 1) // C
    bd_k = r_kk == c_kk
    r_kv = lax.broadcasted_iota(jnp.int32, (GLA_KW, GLA_WIDTH), 0) // C
    c_kv = lax.broadcasted_iota(jnp.int32, (GLA_KW, GLA_WIDTH), 1) // GLA_DV
    bd_v = r_kv == c_kv
    causal = (lax.broadcasted_iota(jnp.int32, (C, GLA_KW), 1) % C
              <= lax.broadcasted_iota(jnp.int32, (C, GLA_KW), 0))
    ltri = ltri_ref[...]

    for hf in range(GLA_TILE // GLA_HALF):
        base = hf * GLA_HALF
        g = la_ref[base:base + GLA_HALF, :]
        g_hi = g.astype(BF16)
        g_lo = (g - g_hi.astype(F32)).astype(BF16)
        b = (jnp.dot(ltri, g_hi, preferred_element_type=F32)
             + jnp.dot(ltri, g_lo, preferred_element_type=F32))
        for c in range(GLA_HALF // C):
            lo = base + c * C
            bc = b[c * C:(c + 1) * C, :]
            bl = bc[C - 1:C, :]
            qc = q_ref[lo:lo + C, :].astype(F32)
            kc = k_ref[lo:lo + C, :].astype(F32)
            vc = v_ref[lo:lo + C, :]
            qd = (qc * jnp.exp(bc)).astype(BF16)
            ki = (kc * jnp.exp(-bc)).astype(BF16)
            kt = kc * jnp.exp(bl - bc)
            kbd = jnp.where(bd_k, jnp.tile(ki, (GLA_HEADS, 1)), jnp.zeros((), BF16))
            a = lax.dot_general(qd, kbd, (((1,), (1,)), ((), ())), preferred_element_type=F32)
            a = jnp.where(causal, a, 0.0).astype(BF16)
            vbd = jnp.where(bd_v, jnp.tile(vc, (GLA_HEADS, 1)), jnp.zeros((), BF16))
            s_prev = s_ref[...]
            sbd = jnp.where(bd_v, jnp.tile(s_prev.astype(BF16), (1, GLA_HEADS)), jnp.zeros((), BF16))
            o = (jnp.dot(a, vbd, preferred_element_type=F32)
                 + jnp.dot(qd, sbd, preferred_element_type=F32))

            kt_t = kt.T.astype(BF16)
            dec = jnp.exp(bc.T[:, C - 1:C])
            u = jnp.concatenate(
                [jnp.dot(kt_t[h * GLA_DK:(h + 1) * GLA_DK, :], vc[:, h * GLA_DV:(h + 1) * GLA_DV],
                         preferred_element_type=F32) for h in range(GLA_HEADS)], axis=0)
            s_ref[...] = dec * s_prev + u

            gate = r_ref[lo:lo + C, :].astype(F32)
            for h in range(GLA_HEADS):
                hs = slice(h * GLA_DV, (h + 1) * GLA_DV)
                oh = _rms(o[:, hs], gn_ref[:, hs])
                o_ref[lo:lo + C, hs] = (oh * gate[:, hs]).astype(BF16)


def _gla(gq, gk, gv, la, gr, gn, ltri, batch):
    T = gq.shape[0]
    tiles = T // batch // GLA_TILE
    row = lambda n: pl.BlockSpec((GLA_TILE, n), lambda b, t: (b * tiles + t, 0))
    return pl.pallas_call(
        _gla_kernel, grid=(batch, tiles),
        in_specs=[row(GLA_KW), row(GLA_KW), row(GLA_WIDTH), row(GLA_KW), row(GLA_WIDTH),
                  _const_spec(gn.shape), _const_spec(ltri.shape)],
        out_specs=row(GLA_WIDTH),
        out_shape=jax.ShapeDtypeStruct((T, GLA_WIDTH), BF16),
        scratch_shapes=[pltpu.VMEM((GLA_KW, GLA_DV), F32)],
        compiler_params=pltpu.CompilerParams(dimension_semantics=("arbitrary", "arbitrary"),
                                             vmem_limit_bytes=VMEM_LIMIT_BYTES),
        name="gla")(gq, gk, gv, la, gr, gn, ltri)


def _attn_kernel(q_ref, k_ref, v_ref, o_ref, xf_ref, *scratch):
    B = ATT_BLOCK
    T = ATT_SUPER
    nd = len(ATT_DILATIONS)
    qd = dict(zip(ATT_DILATIONS[1:], scratch[:nd - 1]))
    kc = dict(zip(ATT_DILATIONS, scratch[nd - 1:2 * nd - 1]))
    vc = dict(zip(ATT_DILATIONS, scratch[2 * nd - 1:3 * nd - 1]))
    acc = scratch[3 * nd - 1:]
    acc_o = dict(zip(ATT_DILATIONS, acc[0::3]))
    acc_d = dict(zip(ATT_DILATIONS, acc[1::3]))
    acc_m = dict(zip(ATT_DILATIONS, acc[2::3]))
    tile = pl.program_id(2)

    def stage(src_ref, dst, halo):
        xf_ref[...] = src_ref[...].astype(F32)
        for d in ATT_DILATIONS[1:]:
            n = T // d
            for r in range(d):
                lo_row = r * (n + halo) + halo
                dst[d][lo_row:lo_row + n, :] = xf_ref[pl.ds(r, n, stride=d), :].astype(BF16)

    @pl.when(tile == 0)
    def _():
        zeros = jnp.zeros((B, LANES), BF16)
        for d in ATT_DILATIONS:
            for r in range(d):
                row = r * (T // d + B)
                kc[d][row:row + B, :] = zeros
                vc[d][row:row + B, :] = zeros

    stage(q_ref, qd, 0)
    stage(k_ref, kc, B)
    stage(v_ref, vc, B)
    kc[1][B:, :] = k_ref[...]
    vc[1][B:, :] = v_ref[...]

    qi = lax.broadcasted_iota(jnp.int32, (B, 2 * B), 0)
    ki = lax.broadcasted_iota(jnp.int32, (B, 2 * B), 1)
    band = (ki >= qi) & (ki <= qi + B)
    bias_band = jnp.where(band, 0.0, NEG).astype(F32)
    bias_first = jnp.where(band & ((ki >= B) | (tile > 0)), 0.0, NEG).astype(F32)
    lane = lax.broadcasted_iota(jnp.int32, (B, LANES), 1)
    lo = lane < ATT_HD
    zero_bf = jnp.zeros((), BF16)
    ones_v = jnp.ones((2 * B, LANES), BF16)

    def block(q_src, q_row, d, k_row, bias, rows_out):
        qp = q_src[q_row:q_row + B, :]
        kk = kc[d][k_row:k_row + 2 * B, :]
        vv = vc[d][k_row:k_row + 2 * B, :]
        qs = jnp.concatenate([jnp.where(lo, qp, zero_bf), jnp.where(lo, zero_bf, qp)], axis=0)
        s = lax.dot_general(qs, kk, (((1,), (1,)), ((), ())), preferred_element_type=F32)
        s = s + jnp.concatenate([bias, bias], axis=0)
        m = jnp.max(s, axis=-1, keepdims=True)
        p = jnp.exp2(s - m).astype(BF16)
        pv = jnp.dot(p, jnp.concatenate([vv, ones_v], axis=1), preferred_element_type=F32)
        acc_o[d][rows_out, :] = jnp.where(lo, pv[:B, :LANES], pv[B:, :LANES])
        acc_d[d][rows_out, :] = jnp.where(lo, pv[:B, LANES:], pv[B:, LANES:])
        acc_m[d][rows_out, :] = jnp.where(lo, m[:B], m[B:])

    for d in ATT_DILATIONS:
        n = T // d
        for r in range(d):
            for jb in range(n // B):
                q_src, q_row = (q_ref, jb * B) if d == 1 else (qd[d], r * n + jb * B)
                rows_out = pl.ds(r + d * B * jb, B, stride=d) if d > 1 else pl.ds(jb * B, B)
                block(q_src, q_row, d, r * (n + B) + jb * B, bias_first if jb == 0 else bias_band,
                      rows_out)

    for c in range(T // ATT_COMBINE_ROWS):
        rows = slice(c * ATT_COMBINE_ROWS, (c + 1) * ATT_COMBINE_ROWS)
        ms = [acc_m[d][rows, :] for d in ATT_DILATIONS]
        mx = functools.reduce(jnp.maximum, ms)
        es = [jnp.exp2(m - mx) for m in ms]
        num = sum(e * acc_o[d][rows, :] for e, d in zip(es, ATT_DILATIONS))
        den = sum(e * acc_d[d][rows, :] for e, d in zip(es, ATT_DILATIONS))
        o_ref[rows, :] = (num / den).astype(o_ref.dtype)

    for d in ATT_DILATIONS:
        n = T // d
        for r in range(d):
            row = r * (n + B)
            kc[d][row:row + B, :] = kc[d][row + n:row + n + B, :]
            vc[d][row:row + B, :] = vc[d][row + n:row + n + B, :]


def _attn(q, k, v, batch, seq):
    T = q.shape[0]
    tiles = seq // ATT_SUPER
    spec = pl.BlockSpec((ATT_SUPER, LANES), lambda b, h, t: (b * tiles + t, h))
    rows = lambda n, dt: pltpu.VMEM((n, LANES), dt)
    kv_rows = [ATT_SUPER + d * ATT_BLOCK for d in ATT_DILATIONS]
    scratch = ([rows(ATT_SUPER, F32)] + [rows(ATT_SUPER, BF16) for _ in ATT_DILATIONS[1:]]
               + [rows(n, BF16) for n in kv_rows] + [rows(n, BF16) for n in kv_rows]
               + [rows(ATT_SUPER, F32) for _ in range(3 * len(ATT_DILATIONS))])
    return pl.pallas_call(
        _attn_kernel, grid=(batch, ATT_WIDTH // LANES, tiles), in_specs=[spec, spec, spec],
        out_specs=spec, out_shape=jax.ShapeDtypeStruct((T, ATT_WIDTH), BF16), scratch_shapes=scratch,
        compiler_params=pltpu.CompilerParams(
            dimension_semantics=("arbitrary", "arbitrary", "arbitrary"),
            vmem_limit_bytes=VMEM_LIMIT_BYTES),
        name="attn")(q, k, v)


def _out_ffn_kernel(x1_ref, og_ref, oa_ref, ga_ref, wog_ref, woa_ref, g2_ref, w1_ref, w3_ref, w2_ref,
                    gf_ref, out_ref):
    oa = _rms(oa_ref[...].astype(F32), ga_ref[...]).astype(BF16)
    x2 = (x1_ref[...]
          + jnp.dot(og_ref[...], wog_ref[...], preferred_element_type=F32)
          + jnp.dot(oa, woa_ref[...], preferred_element_type=F32))
    h = _rms(x2, g2_ref[...]).astype(BF16)
    x3 = x2 + 0.5 * _swiglu_half_step(h, w1_ref, w3_ref, w2_ref)
    out_ref[...] = _rms(x3, gf_ref[...])


def _out_ffn(x1, og, oa, ga, wog, woa, g2, w1, w3, w2, gf):
    T = x1.shape[0]
    tm = TOKEN_TILE
    row = lambda n: pl.BlockSpec((tm, n), lambda i: (i, 0))
    return pl.pallas_call(
        _out_ffn_kernel, grid=(T // tm,),
        in_specs=[row(D_MODEL), row(GLA_WIDTH), row(ATT_WIDTH), _const_spec(ga.shape),
                  _const_spec(wog.shape), _const_spec(woa.shape), _const_spec(g2.shape),
                  _const_spec(w1.shape), _const_spec(w3.shape), _const_spec(w2.shape),
                  _const_spec(gf.shape)],
        out_specs=row(D_MODEL),
        out_shape=jax.ShapeDtypeStruct((T, D_MODEL), F32),
        compiler_params=pltpu.CompilerParams(dimension_semantics=("arbitrary",),
                                             vmem_limit_bytes=VMEM_LIMIT_BYTES),
        name="out_ffn")(x1, og, oa, ga, wog, woa, g2, w1, w3, w2, gf)


def _rope_tables():
    j = np.arange(LANES) % ATT_HD
    half = ROT_DIM // 2
    inv_freq = ROPE_THETA ** (-np.arange(0, ROT_DIM, 2, dtype=np.float32) / ROT_DIM)
    freq = np.where(j < ROT_DIM, inv_freq[j % half], 0.0).astype(np.float32)
    sgn1 = np.where(j < half, -1.0, 0.0).astype(np.float32)
    sgn2 = np.where((j >= half) & (j < ROT_DIM), 1.0, 0.0).astype(np.float32)
    return freq[None, :], sgn1[None, :], sgn2[None, :]


def _chunk_tril():
    i = np.arange(GLA_HALF)
    same = (i[:, None] // GLA_CHUNK) == (i[None, :] // GLA_CHUNK)
    return jnp.asarray((same & (i[None, :] <= i[:, None])).astype(np.float32), dtype=BF16)


def kernel(x, positions, ffn1_norm, ffn1_w1, ffn1_w3, ffn1_w2, mix_norm, w_in, gla_w_a2, gla_b_a,
           gla_out_norm, att_out_norm, w_out, ffn2_norm, ffn2_w1, ffn2_w3, ffn2_w2, final_norm):
    batch, seq, _ = x.shape
    T = batch * seq
    depth = ffn1_norm.shape[0]
    assert depth == 1, "the final norm is fused into the single layer's last kernel"
    freq, sgn1, sgn2 = (jnp.asarray(t) for t in _rope_tables())
    ltri = _chunk_tril()
    pos = positions.astype(F32).reshape(T, 1)
    xs = x.reshape(T, D_MODEL)

    o_gla_end = 2 * GLA_KW + 2 * GLA_WIDTH
    o_ga_end = o_gla_end + GLA_RANK
    for l in range(depth):
        wi = w_in[l].astype(BF16)
        wgla = wi[:, :o_gla_end]
        wga = jnp.pad(wi[:, o_gla_end:o_ga_end], ((0, 0), (0, LANES - GLA_RANK)))
        watt = wi[:, o_ga_end:]
        wa2 = jnp.pad(gla_w_a2[l].astype(BF16), ((0, LANES - GLA_RANK), (0, 0)))
        x1, gq, gk, gv, gr, la, aq, ak, av = _ffn_proj(
            xs, pos, ffn1_norm[l][None, :], ffn1_w1[l].astype(BF16), ffn1_w3[l].astype(BF16),
            ffn1_w2[l].astype(BF16), mix_norm[l][None, :], wgla, wga, wa2, gla_b_a[l][None, :], watt,
            freq, sgn1, sgn2)

        o_gla = _gla(gq, gk, gv, la, gr, gla_out_norm[l][None, :], ltri, batch)

        o_att = _attn(aq, ak, av, batch, seq)

        wo = w_out[l].astype(BF16)
        xs = _out_ffn(x1, o_gla, o_att, att_out_norm[l][None, :], wo[:GLA_WIDTH], wo[GLA_WIDTH:],
                      ffn2_norm[l][None, :], ffn2_w1[l].astype(BF16), ffn2_w3[l].astype(BF16),
                      ffn2_w2[l].astype(BF16), final_norm[None, :])
    return xs.reshape(batch, seq, D_MODEL)
```

```python
import functools
import math

import jax
import jax.numpy as jnp
import numpy as np
from jax import lax
from jax.experimental import pallas as pl
from jax.experimental.pallas import tpu as pltpu

F32 = jnp.float32
BF16 = jnp.bfloat16

D_MODEL = 1024
D_FF = 2816
GLA_WIDTH = 512
GLA_HEADS = 4
GLA_DV = 128
GLA_DK = 64
GLA_KW = GLA_HEADS * GLA_DK
GLA_RANK = 16
GLA_TAU = 16.0
GLA_CHUNK = 64
ATT_WIDTH = 512
ATT_HEADS = 8
ATT_HD = 64
ROT_DIM = 16
ROPE_THETA = 500000.0
DILATED_PATTERNS = ((128, 1), (512, 4), (2048, 16))
ATT_BLOCK = 128
EPS = 1e-6

LANES = 128
VMEM_LIMIT_BYTES = 56 * 1024 * 1024

TOKEN_TILE = 512
MXU_TILE = 256
FF_SPLITS = (0, (D_FF // MXU_TILE // 2) * MXU_TILE, D_FF)
assert D_FF % MXU_TILE == 0
GLA_TILE = 512
GLA_HALF = 256
ATT_DILATIONS = tuple(sorted(d for _, d in DILATED_PATTERNS))
ATT_SUPER = max(ATT_DILATIONS) * ATT_BLOCK
ATT_COMBINE_ROWS = 256
NEG = -1e30
LOG2E = math.log2(math.e)

assert all(w // d == ATT_BLOCK for w, d in DILATED_PATTERNS)
assert ATT_DILATIONS[0] == 1 and all(ATT_SUPER % (d * ATT_BLOCK) == 0 for d in ATT_DILATIONS)


def _const_spec(shape):
    nd = len(shape)
    return pl.BlockSpec(shape, lambda *_: (0,) * nd, pipeline_mode=pl.Buffered(1))


def _rms(x, g):
    return x * lax.rsqrt(jnp.mean(x * x, axis=-1, keepdims=True) + EPS) * g


def _swiglu_half_step(h, w1_ref, w3_ref, w2_ref):
    acc = None
    for lo, hi in zip(FF_SPLITS[:-1], FF_SPLITS[1:]):
        sl = slice(lo, hi)
        a = jnp.dot(h, w1_ref[:, sl], preferred_element_type=F32)
        b = jnp.dot(h, w3_ref[:, sl], preferred_element_type=F32)
        g = (a / (1.0 + jnp.exp(-a)) * b).astype(BF16)
        part = jnp.dot(g, w2_ref[sl, :], preferred_element_type=F32)
        acc = part if acc is None else acc + part
    return acc


def _ffn_proj_kernel(x_ref, pos_ref, g1_ref, w1_ref, w3_ref, w2_ref, gmix_ref, wgla_ref, wga_ref,
                     wa2_ref, ba_ref, watt_ref, freq_ref, sgn1_ref, sgn2_ref, *rest):
    n_late = (len(rest) - N_PROJ_OUTPUTS) // 2
    late_in = rest[:n_late]
    x1_ref, gq_ref, gk_ref, gv_ref, gr_ref, la_ref, aq_ref, ak_ref, av_ref = (
        rest[n_late:n_late + N_PROJ_OUTPUTS])
    late_out = rest[n_late + N_PROJ_OUTPUTS:]
    for src, dst in zip(late_in, late_out):
        dst[...] = src[...].astype(BF16)

    x = x_ref[...]
    h = _rms(x, g1_ref[...]).astype(BF16)
    x1 = x + 0.5 * _swiglu_half_step(h, w1_ref, w3_ref, w2_ref)
    x1_ref[...] = x1

    h2 = _rms(x1, gmix_ref[...]).astype(BF16)
    pa = jnp.dot(h2, watt_ref[...], preferred_element_type=F32)
    av_ref[...] = pa[:, 2 * ATT_WIDTH:].astype(BF16)

    ang = pos_ref[...] * freq_ref[...]
    cos = jnp.cos(ang)
    sin = jnp.sin(ang)
    s1 = sin * sgn1_ref[...]
    s2 = sin * sgn2_ref[...]
    half = ROT_DIM // 2
    for off, ref, scale in ((0, aq_ref, ATT_HD ** -0.5 * LOG2E), (ATT_WIDTH, ak_ref, 1.0)):
        for cg in range(ATT_WIDTH // LANES):
            t = pa[:, off + cg * LANES: off + (cg + 1) * LANES]
            r = t * cos + pltpu.roll(t, LANES - half, 1) * s1 + pltpu.roll(t, half, 1) * s2
            ref[:, cg * LANES:(cg + 1) * LANES] = (r * scale).astype(BF16)

    ga = jnp.dot(h2, wga_ref[...], preferred_element_type=F32).astype(BF16)
    z = jnp.dot(ga, wa2_ref[...], preferred_element_type=F32) + ba_ref[...]
    la_ref[...] = (jnp.minimum(z, 0.0) - jnp.log(1.0 + jnp.exp(-jnp.abs(z)))) * (1.0 / GLA_TAU)

    pg = jnp.dot(h2, wgla_ref[...], preferred_element_type=F32)
    gq_ref[...] = (pg[:, :GLA_KW] * (GLA_DK ** -0.5)).astype(BF16)
    gk_ref[...] = pg[:, GLA_KW:2 * GLA_KW].astype(BF16)
    gv_ref[...] = pg[:, 2 * GLA_KW:2 * GLA_KW + GLA_WIDTH].astype(BF16)
    gr = pg[:, 2 * GLA_KW + GLA_WIDTH:]
    gr_ref[...] = (gr / (1.0 + jnp.exp(-gr))).astype(BF16)


N_PROJ_OUTPUTS = 9
BF16_SUBLANES = 16


def _late_chunk_spec(shape, steps):
    rows, cols = shape
    share = 1 if (rows // steps) % BF16_SUBLANES == 0 else 2
    assert rows % (steps // share) == 0 and (rows * share // steps) % BF16_SUBLANES == 0
    return pl.BlockSpec((rows * share // steps, cols), lambda i: (i // share, 0))


def _ffn_proj(x2d, pos, g1, w1, w3, w2, gmix, wgla, wga, wa2, ba, watt, freq, sgn1, sgn2, late):
    T = x2d.shape[0]
    tm = TOKEN_TILE
    steps = T // tm
    row = lambda n: pl.BlockSpec((tm, n), lambda i: (i, 0))
    late_specs = [_late_chunk_spec(w.shape, steps) for w in late]
    in_specs = [row(D_MODEL), row(1), _const_spec(g1.shape), _const_spec(w1.shape), _const_spec(w3.shape),
                _const_spec(w2.shape), _const_spec(gmix.shape), _const_spec(wgla.shape),
                _const_spec(wga.shape), _const_spec(wa2.shape), _const_spec(ba.shape),
                _const_spec(watt.shape), _const_spec(freq.shape), _const_spec(sgn1.shape),
                _const_spec(sgn2.shape)] + late_specs
    out_shape = [jax.ShapeDtypeStruct((T, D_MODEL), F32),
                 jax.ShapeDtypeStruct((T, GLA_KW), BF16), jax.ShapeDtypeStruct((T, GLA_KW), BF16),
                 jax.ShapeDtypeStruct((T, GLA_WIDTH), BF16), jax.ShapeDtypeStruct((T, GLA_WIDTH), BF16),
                 jax.ShapeDtypeStruct((T, GLA_KW), F32),
                 jax.ShapeDtypeStruct((T, ATT_WIDTH), BF16), jax.ShapeDtypeStruct((T, ATT_WIDTH), BF16),
                 jax.ShapeDtypeStruct((T, ATT_WIDTH), BF16)]
    assert len(out_shape) == N_PROJ_OUTPUTS
    out_specs = [row(s.shape[1]) for s in out_shape] + late_specs
    out_shape += [jax.ShapeDtypeStruct(w.shape, BF16) for w in late]
    res = pl.pallas_call(
        _ffn_proj_kernel, grid=(steps,), in_specs=in_specs, out_specs=out_specs, out_shape=out_shape,
        compiler_params=pltpu.CompilerParams(dimension_semantics=("arbitrary",),
                                             vmem_limit_bytes=VMEM_LIMIT_BYTES),
        name="ffn_proj")(x2d, pos, g1, w1, w3, w2, gmix, wgla, wga, wa2, ba, watt, freq, sgn1, sgn2,
                         *late)
    return res[:N_PROJ_OUTPUTS], res[N_PROJ_OUTPUTS:]


def _gla_kernel(q_ref, k_ref, v_ref, la_ref, r_ref, gn_ref, ltri_ref, o_ref, s_ref):
    @pl.when(pl.program_id(1) == 0)
    def _():
        s_ref[...] = jnp.zeros_like(s_ref)

    C = GLA_CHUNK
    r_kk = lax.broadcasted_iota(jnp.int32, (GLA_KW, GLA_KW), 0) // C
    c_kk = lax.broadcasted_iota(jnp.int32, (GLA_KW, GLA_KW), 1) // C
    bd_k = r_kk == c_kk
    r_kv = lax.broadcasted_iota(jnp.int32, (GLA_KW, GLA_WIDTH), 0) // C
    c_kv = lax.broadcasted_iota(jnp.int32, (GLA_KW, GLA_WIDTH), 1) // GLA_DV
    bd_v = r_kv == c_kv
    causal = (lax.broadcasted_iota(jnp.int32, (C, GLA_KW), 1) % C
              <= lax.broadcasted_iota(jnp.int32, (C, GLA_KW), 0))
    ltri = ltri_ref[...]

    for hf in range(GLA_TILE // GLA_HALF):
        base = hf * GLA_HALF
        g = la_ref[base:base + GLA_HALF, :]
        g_hi = g.astype(BF16)
        g_lo = (g - g_hi.astype(F32)).astype(BF16)
        b = (jnp.dot(ltri, g_hi, preferred_element_type=F32)
             + jnp.dot(ltri, g_lo, preferred_element_type=F32))
        for c in range(GLA_HALF // C):
            lo = base + c * C
            bc = b[c * C:(c + 1) * C, :]
            bl = bc[C - 1:C, :]
            qc = q_ref[lo:lo + C, :].astype(F32)
            kc = k_ref[lo:lo + C, :].astype(F32)
            vc = v_ref[lo:lo + C, :]
            qd = (qc * jnp.exp(bc)).astype(BF16)
            ki = (kc * jnp.exp(-bc)).astype(BF16)
            kt = kc * jnp.exp(bl - bc)
            kbd = jnp.where(bd_k, jnp.tile(ki, (GLA_HEADS, 1)), jnp.zeros((), BF16))
            a = lax.dot_general(qd, kbd, (((1,), (1,)), ((), ())), preferred_element_type=F32)
            a = jnp.where(causal, a, 0.0).astype(BF16)
            vbd = jnp.where(bd_v, jnp.tile(vc, (GLA_HEADS, 1)), jnp.zeros((), BF16))
            s_prev = s_ref[...]
            sbd = jnp.where(bd_v, jnp.tile(s_prev.astype(BF16), (1, GLA_HEADS)), jnp.zeros((), BF16))
            o = (jnp.dot(a, vbd, preferred_element_type=F32)
                 + jnp.dot(qd, sbd, preferred_element_type=F32))

            kt_t = kt.T.astype(BF16)
            dec = jnp.exp(bc.T[:, C - 1:C])
            u = jnp.concatenate(
                [jnp.dot(kt_t[h * GLA_DK:(h + 1) * GLA_DK, :], vc[:, h * GLA_DV:(h + 1) * GLA_DV],
                         preferred_element_type=F32) for h in range(GLA_HEADS)], axis=0)
            s_ref[...] = dec * s_prev + u

            gate = r_ref[lo:lo + C, :].astype(F32)
            for h in range(GLA_HEADS):
                hs = slice(h * GLA_DV, (h + 1) * GLA_DV)
                oh = _rms(o[:, hs], gn_ref[:, hs])
                o_ref[lo:lo + C, hs] = (oh * gate[:, hs]).astype(BF16)


def _gla(gq, gk, gv, la, gr, gn, ltri, batch):
    T = gq.shape[0]
    tiles = T // batch // GLA_TILE
    row = lambda n: pl.BlockSpec((GLA_TILE, n), lambda b, t: (b * tiles + t, 0))
    return pl.pallas_call(
        _gla_kernel, grid=(batch, tiles),
        in_specs=[row(GLA_KW), row(GLA_KW), row(GLA_WIDTH), row(GLA_KW), row(GLA_WIDTH),
                  _const_spec(gn.shape), _const_spec(ltri.shape)],
        out_specs=row(GLA_WIDTH),
        out_shape=jax.ShapeDtypeStruct((T, GLA_WIDTH), BF16),
        scratch_shapes=[pltpu.VMEM((GLA_KW, GLA_DV), F32)],
        compiler_params=pltpu.CompilerParams(dimension_semantics=("arbitrary", "arbitrary"),
                                             vmem_limit_bytes=VMEM_LIMIT_BYTES),
        name="gla")(gq, gk, gv, la, gr, gn, ltri)


def _attn_kernel(q_ref, k_ref, v_ref, o_ref, xf_ref, *scratch):
    B = ATT_BLOCK
    T = ATT_SUPER
    nd = len(ATT_DILATIONS)
    qd = dict(zip(ATT_DILATIONS[1:], scratch[:nd - 1]))
    kc = dict(zip(ATT_DILATIONS, scratch[nd - 1:2 * nd - 1]))
    vc = dict(zip(ATT_DILATIONS, scratch[2 * nd - 1:3 * nd - 1]))
    acc = scratch[3 * nd - 1:]
    acc_o = dict(zip(ATT_DILATIONS, acc[0::3]))
    acc_d = dict(zip(ATT_DILATIONS, acc[1::3]))
    acc_m = dict(zip(ATT_DILATIONS, acc[2::3]))
    tile = pl.program_id(2)

    def stage(src_ref, dst, halo):
        xf_ref[...] = src_ref[...].astype(F32)
        for d in ATT_DILATIONS[1:]:
            n = T // d
            for r in range(d):
                lo_row = r * (n + halo) + halo
                dst[d][lo_row:lo_row + n, :] = xf_ref[pl.ds(r, n, stride=d), :].astype(BF16)

    @pl.when(tile == 0)
    def _():
        zeros = jnp.zeros((B, LANES), BF16)
        for d in ATT_DILATIONS:
            for r in range(d):
                row = r * (T // d + B)
                kc[d][row:row + B, :] = zeros
                vc[d][row:row + B, :] = zeros

    stage(q_ref, qd, 0)
    stage(k_ref, kc, B)
    stage(v_ref, vc, B)
    kc[1][B:, :] = k_ref[...]
    vc[1][B:, :] = v_ref[...]

    qi = lax.broadcasted_iota(jnp.int32, (B, 2 * B), 0)
    ki = lax.broadcasted_iota(jnp.int32, (B, 2 * B), 1)
    band = (ki >= qi) & (ki <= qi + B)
    bias_band = jnp.where(band, 0.0, NEG).astype(F32)
    bias_first = jnp.where(band & ((ki >= B) | (tile > 0)), 0.0, NEG).astype(F32)
    lane = lax.broadcasted_iota(jnp.int32, (B, LANES), 1)
    lo = lane < ATT_HD
    zero_bf = jnp.zeros((), BF16)
    ones_v = jnp.ones((2 * B, LANES), BF16)

    def block(q_src, q_row, d, k_row, bias, rows_out):
        qp = q_src[q_row:q_row + B, :]
        kk = kc[d][k_row:k_row + 2 * B, :]
        vv = vc[d][k_row:k_row + 2 * B, :]
        qs = jnp.concatenate([jnp.where(lo, qp, zero_bf), jnp.where(lo, zero_bf, qp)], axis=0)
        s = lax.dot_general(qs, kk, (((1,), (1,)), ((), ())), preferred_element_type=F32)
        s = s + jnp.concatenate([bias, bias], axis=0)
        m = jnp.max(s, axis=-1, keepdims=True)
        p = jnp.exp2(s - m).astype(BF16)
        pv = jnp.dot(p, jnp.concatenate([vv, ones_v], axis=1), preferred_element_type=F32)
        acc_o[d][rows_out, :] = jnp.where(lo, pv[:B, :LANES], pv[B:, :LANES])
        acc_d[d][rows_out, :] = jnp.where(lo, pv[:B, LANES:], pv[B:, LANES:])
        acc_m[d][rows_out, :] = jnp.where(lo, m[:B], m[B:])

    for d in ATT_DILATIONS:
        n = T // d
        for r in range(d):
            for jb in range(n // B):
                q_src, q_row = (q_ref, jb * B) if d == 1 else (qd[d], r * n + jb * B)
                rows_out = pl.ds(r + d * B * jb, B, stride=d) if d > 1 else pl.ds(jb * B, B)
                block(q_src, q_row, d, r * (n + B) + jb * B, bias_first if jb == 0 else bias_band,
                      rows_out)

    for c in range(T // ATT_COMBINE_ROWS):
        rows = slice(c * ATT_COMBINE_ROWS, (c + 1) * ATT_COMBINE_ROWS)
        ms = [acc_m[d][rows, :] for d in ATT_DILATIONS]
        mx = functools.reduce(jnp.maximum, ms)
        es = [jnp.exp2(m - mx) for m in ms]
        num = sum(e * acc_o[d][rows, :] for e, d in zip(es, ATT_DILATIONS))
        den = sum(e * acc_d[d][rows, :] for e, d in zip(es, ATT_DILATIONS))
        o_ref[rows, :] = (num / den).astype(o_ref.dtype)

    for d in ATT_DILATIONS:
        n = T // d
        for r in range(d):
            row = r * (n + B)
            kc[d][row:row + B, :] = kc[d][row + n:row + n + B, :]
            vc[d][row:row + B, :] = vc[d][row + n:row + n + B, :]


def _attn(q, k, v, batch, seq):
    T = q.shape[0]
    tiles = seq // ATT_SUPER
    spec = pl.BlockSpec((ATT_SUPER, LANES), lambda b, h, t: (b * tiles + t, h))
    rows = lambda n, dt: pltpu.VMEM((n, LANES), dt)
    kv_rows = [ATT_SUPER + d * ATT_BLOCK for d in ATT_DILATIONS]
    scratch = ([rows(ATT_SUPER, F32)] + [rows(ATT_SUPER, BF16) for _ in ATT_DILATIONS[1:]]
               + [rows(n, BF16) for n in kv_rows] + [rows(n, BF16) for n in kv_rows]
               + [rows(ATT_SUPER, F32) for _ in range(3 * len(ATT_DILATIONS))])
    return pl.pallas_call(
        _attn_kernel, grid=(batch, ATT_WIDTH // LANES, tiles), in_specs=[spec, spec, spec],
        out_specs=spec, out_shape=jax.ShapeDtypeStruct((T, ATT_WIDTH), BF16), scratch_shapes=scratch,
        compiler_params=pltpu.CompilerParams(
            dimension_semantics=("arbitrary", "arbitrary", "arbitrary"),
            vmem_limit_bytes=VMEM_LIMIT_BYTES),
        name="attn")(q, k, v)


def _out_ffn_kernel(x1_ref, og_ref, oa_ref, ga_ref, wo_ref, g2_ref, w1_ref, w3_ref, w2_ref, gf_ref,
                    out_ref):
    oa = _rms(oa_ref[...].astype(F32), ga_ref[...]).astype(BF16)
    x2 = (x1_ref[...]
          + jnp.dot(og_ref[...], wo_ref[:GLA_WIDTH, :], preferred_element_type=F32)
          + jnp.dot(oa, wo_ref[GLA_WIDTH:, :], preferred_element_type=F32))
    h = _rms(x2, g2_ref[...]).astype(BF16)
    x3 = x2 + 0.5 * _swiglu_half_step(h, w1_ref, w3_ref, w2_ref)
    out_ref[...] = _rms(x3, gf_ref[...])


def _out_ffn(x1, og, oa, ga, wo, g2, w1, w3, w2, gf):
    T = x1.shape[0]
    tm = TOKEN_TILE
    row = lambda n: pl.BlockSpec((tm, n), lambda i: (i, 0))
    return pl.pallas_call(
        _out_ffn_kernel, grid=(T // tm,),
        in_specs=[row(D_MODEL), row(GLA_WIDTH), row(ATT_WIDTH), _const_spec(ga.shape),
                  _const_spec(wo.shape), _const_spec(g2.shape),
                  _const_spec(w1.shape), _const_spec(w3.shape), _const_spec(w2.shape),
                  _const_spec(gf.shape)],
        out_specs=row(D_MODEL),
        out_shape=jax.ShapeDtypeStruct((T, D_MODEL), F32),
        compiler_params=pltpu.CompilerParams(dimension_semantics=("arbitrary",),
                                             vmem_limit_bytes=VMEM_LIMIT_BYTES),
        name="out_ffn")(x1, og, oa, ga, wo, g2, w1, w3, w2, gf)


def _rope_tables():
    j = np.arange(LANES) % ATT_HD
    half = ROT_DIM // 2
    inv_freq = ROPE_THETA ** (-np.arange(0, ROT_DIM, 2, dtype=np.float32) / ROT_DIM)
    freq = np.where(j < ROT_DIM, inv_freq[j % half], 0.0).astype(np.float32)
    sgn1 = np.where(j < half, -1.0, 0.0).astype(np.float32)
    sgn2 = np.where((j >= half) & (j < ROT_DIM), 1.0, 0.0).astype(np.float32)
    return freq[None, :], sgn1[None, :], sgn2[None, :]


def _chunk_tril():
    i = np.arange(GLA_HALF)
    same = (i[:, None] // GLA_CHUNK) == (i[None, :] // GLA_CHUNK)
    return jnp.asarray((same & (i[None, :] <= i[:, None])).astype(np.float32), dtype=BF16)


def kernel(x, positions, ffn1_norm, ffn1_w1, ffn1_w3, ffn1_w2, mix_norm, w_in, gla_w_a2, gla_b_a,
           gla_out_norm, att_out_norm, w_out, ffn2_norm, ffn2_w1, ffn2_w3, ffn2_w2, final_norm):
    batch, seq, _ = x.shape
    T = batch * seq
    depth = ffn1_norm.shape[0]
    assert depth == 1, "the final norm is fused into the single layer's last kernel"
    freq, sgn1, sgn2 = (jnp.asarray(t) for t in _rope_tables())
    ltri = _chunk_tril()
    pos = positions.astype(F32).reshape(T, 1)
    xs = x.reshape(T, D_MODEL)

    o_gla_end = 2 * GLA_KW + 2 * GLA_WIDTH
    o_ga_end = o_gla_end + GLA_RANK
    for l in range(depth):
        wi = w_in[l]
        wgla = wi[:, :o_gla_end].astype(BF16)
        wga = jnp.pad(wi[:, o_gla_end:o_ga_end].astype(BF16), ((0, 0), (0, LANES - GLA_RANK)))
        watt = wi[:, o_ga_end:].astype(BF16)
        wa2 = jnp.pad(gla_w_a2[l].astype(BF16), ((0, LANES - GLA_RANK), (0, 0)))
        (x1, gq, gk, gv, gr, la, aq, ak, av), (w1b, w3b, w2b, wob) = _ffn_proj(
            xs, pos, ffn1_norm[l][None, :], ffn1_w1[l].astype(BF16), ffn1_w3[l].astype(BF16),
            ffn1_w2[l].astype(BF16), mix_norm[l][None, :], wgla, wga, wa2, gla_b_a[l][None, :], watt,
            freq, sgn1, sgn2, late=(ffn2_w1[l], ffn2_w3[l], ffn2_w2[l], w_out[l]))

        o_gla = _gla(gq, gk, gv, la, gr, gla_out_norm[l][None, :], ltri, batch)

        o_att = _attn(aq, ak, av, batch, seq)

        xs = _out_ffn(x1, o_gla, o_att, att_out_norm[l][None, :], wob, ffn2_norm[l][None, :], w1b, w3b,
                      w2b, final_norm[None, :])
    return xs.reshape(batch, seq, D_MODEL)
```

```python
import functools
import math

import jax
import jax.numpy as jnp
import numpy as np
from jax import lax
from jax.experimental import pallas as pl
from jax.experimental.pallas import tpu as pltpu

F32 = jnp.float32
BF16 = jnp.bfloat16

D_MODEL = 1024
D_FF = 2816
GLA_WIDTH = 512
GLA_HEADS = 4
GLA_DV = 128
GLA_DK = 64
GLA_KW = GLA_HEADS * GLA_DK
GLA_RANK = 16
GLA_TAU = 16.0
GLA_CHUNK = 64
ATT_WIDTH = 512
ATT_HEADS = 8
ATT_HD = 64
ROT_DIM = 16
ROPE_THETA = 500000.0
DILATED_PATTERNS = ((128, 1), (512, 4), (2048, 16))
ATT_BLOCK = 128
EPS = 1e-6

LANES = 128
BF16_SUBLANES = 16
VMEM_LIMIT_BYTES = 56 * 1024 * 1024

TOKEN_TILE = 512
MXU_TILE = 256
FF_SPLITS = (0, (D_FF // MXU_TILE // 2) * MXU_TILE, D_FF)
assert D_FF % MXU_TILE == 0
GLA_HALF = 256
ATT_DILATIONS = tuple(sorted(d for _, d in DILATED_PATTERNS))
ATT_SUPER = max(ATT_DILATIONS) * ATT_BLOCK
ATT_COMBINE_ROWS = 256
NEG = -1e30
LOG2E = math.log2(math.e)

assert all(w // d == ATT_BLOCK for w, d in DILATED_PATTERNS)
assert ATT_DILATIONS[0] == 1 and all(ATT_SUPER % (d * ATT_BLOCK) == 0 for d in ATT_DILATIONS)


def _const_spec(shape):
    nd = len(shape)
    return pl.BlockSpec(shape, lambda *_: (0,) * nd, pipeline_mode=pl.Buffered(1))


def _rms(x, g):
    return x * lax.rsqrt(jnp.mean(x * x, axis=-1, keepdims=True) + EPS) * g


def _swiglu_half_step(h, w1_ref, w3_ref, w2_ref):
    acc = None
    for lo, hi in zip(FF_SPLITS[:-1], FF_SPLITS[1:]):
        sl = slice(lo, hi)
        a = jnp.dot(h, w1_ref[:, sl], preferred_element_type=F32)
        b = jnp.dot(h, w3_ref[:, sl], preferred_element_type=F32)
        g = (a / (1.0 + jnp.exp(-a)) * b).astype(BF16)
        part = jnp.dot(g, w2_ref[sl, :], preferred_element_type=F32)
        acc = part if acc is None else acc + part
    return acc


def _gla_tile(q, k, v, la, gate, gn_ref, ltri, first_of_batch, o_ref, s_ref):
    C = GLA_CHUNK
    r_kk = lax.broadcasted_iota(jnp.int32, (GLA_KW, GLA_KW), 0) // C
    c_kk = lax.broadcasted_iota(jnp.int32, (GLA_KW, GLA_KW), 1) // C
    bd_k = r_kk == c_kk
    r_kv = lax.broadcasted_iota(jnp.int32, (GLA_KW, GLA_WIDTH), 0) // C
    c_kv = lax.broadcasted_iota(jnp.int32, (GLA_KW, GLA_WIDTH), 1) // GLA_DV
    bd_v = r_kv == c_kv
    causal = (lax.broadcasted_iota(jnp.int32, (C, GLA_KW), 1) % C
              <= lax.broadcasted_iota(jnp.int32, (C, GLA_KW), 0))
    s_ref[...] = jnp.where(first_of_batch, 0.0, s_ref[...])

    for hf in range(q.shape[0] // GLA_HALF):
        base = hf * GLA_HALF
        g = la[base:base + GLA_HALF, :]
        g_hi = g.astype(BF16)
        g_lo = (g - g_hi.astype(F32)).astype(BF16)
        b = (jnp.dot(ltri, g_hi, preferred_element_type=F32)
             + jnp.dot(ltri, g_lo, preferred_element_type=F32))
        for c in range(GLA_HALF // C):
            lo = base + c * C
            bc = b[c * C:(c + 1) * C, :]
            bl = bc[C - 1:C, :]
            qc = q[lo:lo + C, :]
            kc = k[lo:lo + C, :]
            vc = v[lo:lo + C, :]
            qd = (qc * jnp.exp(bc)).astype(BF16)
            ki = (kc * jnp.exp(-bc)).astype(BF16)
            kt = kc * jnp.exp(bl - bc)
            kbd = jnp.where(bd_k, jnp.tile(ki, (GLA_HEADS, 1)), jnp.zeros((), BF16))
            a = lax.dot_general(qd, kbd, (((1,), (1,)), ((), ())), preferred_element_type=F32)
            a = jnp.where(causal, a, 0.0).astype(BF16)
            vbd = jnp.where(bd_v, jnp.tile(vc, (GLA_HEADS, 1)), jnp.zeros((), BF16))
            s_prev = s_ref[...]
            sbd = jnp.where(bd_v, jnp.tile(s_prev.astype(BF16), (1, GLA_HEADS)), jnp.zeros((), BF16))
            o = (jnp.dot(a, vbd, preferred_element_type=F32)
                 + jnp.dot(qd, sbd, preferred_element_type=F32))

            kt_t = kt.T.astype(BF16)
            dec = jnp.exp(bc.T[:, C - 1:C])
            u = jnp.concatenate(
                [jnp.dot(kt_t[h * GLA_DK:(h + 1) * GLA_DK, :], vc[:, h * GLA_DV:(h + 1) * GLA_DV],
                         preferred_element_type=F32) for h in range(GLA_HEADS)], axis=0)
            s_ref[...] = dec * s_prev + u

            for h in range(GLA_HEADS):
                hs = slice(h * GLA_DV, (h + 1) * GLA_DV)
                oh = _rms(o[:, hs], gn_ref[:, hs])
                o_ref[lo:lo + C, hs] = (oh * gate[lo:lo + C, hs]).astype(BF16)


N_MIX_OUTPUTS = 5


def _ffn_mix_kernel(tiles_per_batch, x_ref, pos_ref, g1_ref, w1_ref, w3_ref, w2_ref, gmix_ref, wgla_ref,
                    wga_ref, wa2_ref, ba_ref, watt_ref, freq_ref, sgn1_ref, sgn2_ref, gn_ref, ltri_ref,
                    *rest):
    n_late = (len(rest) - N_MIX_OUTPUTS - 2) // 2
    late_in = rest[:n_late]
    x1_ref, og_ref, aq_ref, ak_ref, av_ref = rest[n_late:n_late + N_MIX_OUTPUTS]
    late_out = rest[n_late + N_MIX_OUTPUTS:2 * n_late + N_MIX_OUTPUTS]
    x1s_ref, s_ref = rest[2 * n_late + N_MIX_OUTPUTS:]
    i = pl.program_id(0)
    n = pl.num_programs(0) - 1

    @pl.when(i == 0)
    def _():
        x1s_ref[...] = jnp.zeros_like(x1s_ref)
        s_ref[...] = jnp.zeros_like(s_ref)

    def project_and_mix():
        h2 = _rms(x1s_ref[...], gmix_ref[...]).astype(BF16)
        pa = jnp.dot(h2, watt_ref[...], preferred_element_type=F32)
        av_ref[...] = pa[:, 2 * ATT_WIDTH:].astype(BF16)

        ang = pos_ref[...] * freq_ref[...]
        cos = jnp.cos(ang)
        sin = jnp.sin(ang)
        s1 = sin * sgn1_ref[...]
        s2 = sin * sgn2_ref[...]
        half = ROT_DIM // 2
        for off, ref, scale in ((0, aq_ref, ATT_HD ** -0.5 * LOG2E), (ATT_WIDTH, ak_ref, 1.0)):
            for cg in range(ATT_WIDTH // LANES):
                t = pa[:, off + cg * LANES: off + (cg + 1) * LANES]
                r = t * cos + pltpu.roll(t, LANES - half, 1) * s1 + pltpu.roll(t, half, 1) * s2
                ref[:, cg * LANES:(cg + 1) * LANES] = (r * scale).astype(BF16)

        ga = jnp.dot(h2, wga_ref[...], preferred_element_type=F32).astype(BF16)
        z = jnp.dot(ga, wa2_ref[...], preferred_element_type=F32) + ba_ref[...]
        la = (jnp.minimum(z, 0.0) - jnp.log(1.0 + jnp.exp(-jnp.abs(z)))) * (1.0 / GLA_TAU)

        pg = jnp.dot(h2, wgla_ref[...], preferred_element_type=F32)
        gr = pg[:, 2 * GLA_KW + GLA_WIDTH:]
        first_of_batch = (i - 1) % tiles_per_batch == 0
        _gla_tile(pg[:, :GLA_KW] * (GLA_DK ** -0.5), pg[:, GLA_KW:2 * GLA_KW],
                  pg[:, 2 * GLA_KW:2 * GLA_KW + GLA_WIDTH].astype(BF16), la,
                  gr / (1.0 + jnp.exp(-gr)), gn_ref, ltri_ref[...], first_of_batch, og_ref, s_ref)

    def ffn_half_step():
        for src, dst in zip(late_in, late_out):
            dst[...] = src[...].astype(BF16)
        x = x_ref[...]
        h = _rms(x, g1_ref[...]).astype(BF16)
        x1 = x + 0.5 * _swiglu_half_step(h, w1_ref, w3_ref, w2_ref)
        x1_ref[...] = x1
        x1s_ref[...] = x1

    @pl.when(i < n)
    def _():
        project_and_mix()
        ffn_half_step()

    @pl.when(i == n)
    def _():
        project_and_mix()


def _late_chunk_spec(shape, steps):
    rows, cols = shape
    share = 1 if (rows // steps) % BF16_SUBLANES == 0 else 2
    assert rows % (steps // share) == 0 and (rows * share // steps) % BF16_SUBLANES == 0
    return pl.BlockSpec((rows * share // steps, cols),
                        lambda i: (jnp.minimum(i, steps - 1) // share, 0))


def _ffn_mix(x2d, pos, g1, w1, w3, w2, gmix, wgla, wga, wa2, ba, watt, freq, sgn1, sgn2, gn, ltri,
             late, batch):
    T = x2d.shape[0]
    tm = TOKEN_TILE
    steps = T // tm
    cur = lambda n: pl.BlockSpec((tm, n), lambda i: (jnp.minimum(i, steps - 1), 0))
    prev = lambda n: pl.BlockSpec((tm, n), lambda i: (jnp.maximum(i - 1, 0), 0))
    late_specs = [_late_chunk_spec(w.shape, steps) for w in late]
    consts = (g1, w1, w3, w2, gmix, wgla, wga, wa2, ba, watt, freq, sgn1, sgn2, gn, ltri)
    in_specs = [cur(D_MODEL), prev(1)] + [_const_spec(c.shape) for c in consts] + late_specs
    out_shape = [jax.ShapeDtypeStruct((T, D_MODEL), F32), jax.ShapeDtypeStruct((T, GLA_WIDTH), BF16),
                 jax.ShapeDtypeStruct((T, ATT_WIDTH), BF16), jax.ShapeDtypeStruct((T, ATT_WIDTH), BF16),
                 jax.ShapeDtypeStruct((T, ATT_WIDTH), BF16)]
    assert len(out_shape) == N_MIX_OUTPUTS
    out_specs = [cur(D_MODEL)] + [prev(s.shape[1]) for s in out_shape[1:]] + late_specs
    out_shape += [jax.ShapeDtypeStruct(w.shape, BF16) for w in late]
    res = pl.pallas_call(
        functools.partial(_ffn_mix_kernel, steps // batch), grid=(steps + 1,), in_specs=in_specs,
        out_specs=out_specs, out_shape=out_shape,
        scratch_shapes=[pltpu.VMEM((tm, D_MODEL), F32), pltpu.VMEM((GLA_KW, GLA_DV), F32)],
        compiler_params=pltpu.CompilerParams(dimension_semantics=("arbitrary",),
                                             vmem_limit_bytes=VMEM_LIMIT_BYTES),
        name="ffn_mix")(x2d, pos, *consts, *late)
    return res[:N_MIX_OUTPUTS], res[N_MIX_OUTPUTS:]


def _attn_kernel(q_ref, k_ref, v_ref, o_ref, xf_ref, *scratch):
    B = ATT_BLOCK
    T = ATT_SUPER
    nd = len(ATT_DILATIONS)
    qd = dict(zip(ATT_DILATIONS[1:], scratch[:nd - 1]))
    kc = dict(zip(ATT_DILATIONS, scratch[nd - 1:2 * nd - 1]))
    vc = dict(zip(ATT_DILATIONS, scratch[2 * nd - 1:3 * nd - 1]))
    acc = scratch[3 * nd - 1:]
    acc_o = dict(zip(ATT_DILATIONS, acc[0::3]))
    acc_d = dict(zip(ATT_DILATIONS, acc[1::3]))
    acc_m = dict(zip(ATT_DILATIONS, acc[2::3]))
    tile = pl.program_id(2)

    def stage(src_ref, dst, halo):
        xf_ref[...] = src_ref[...].astype(F32)
        for d in ATT_DILATIONS[1:]:
            n = T // d
            for r in range(d):
                lo_row = r * (n + halo) + halo
                dst[d][lo_row:lo_row + n, :] = xf_ref[pl.ds(r, n, stride=d), :].astype(BF16)

    @pl.when(tile == 0)
    def _():
        zeros = jnp.zeros((B, LANES), BF16)
        for d in ATT_DILATIONS:
            for r in range(d):
                row = r * (T // d + B)
                kc[d][row:row + B, :] = zeros
                vc[d][row:row + B, :] = zeros

    stage(q_ref, qd, 0)
    stage(k_ref, kc, B)
    stage(v_ref, vc, B)
    kc[1][B:, :] = k_ref[...]
    vc[1][B:, :] = v_ref[...]

    qi = lax.broadcasted_iota(jnp.int32, (B, 2 * B), 0)
    ki = lax.broadcasted_iota(jnp.int32, (B, 2 * B), 1)
    band = (ki >= qi) & (ki <= qi + B)
    bias_band = jnp.where(band, 0.0, NEG).astype(F32)
    bias_first = jnp.where(band & ((ki >= B) | (tile > 0)), 0.0, NEG).astype(F32)
    lane = lax.broadcasted_iota(jnp.int32, (B, LANES), 1)
    lo = lane < ATT_HD
    zero_bf = jnp.zeros((), BF16)
    ones_v = jnp.ones((2 * B, LANES), BF16)

    def block(q_src, q_row, d, k_row, bias, rows_out):
        qp = q_src[q_row:q_row + B, :]
        kk = kc[d][k_row:k_row + 2 * B, :]
        vv = vc[d][k_row:k_row + 2 * B, :]
        qs = jnp.concatenate([jnp.where(lo, qp, zero_bf), jnp.where(lo, zero_bf, qp)], axis=0)
        s = lax.dot_general(qs, kk, (((1,), (1,)), ((), ())), preferred_element_type=F32)
        s = s + jnp.concatenate([bias, bias], axis=0)
        m = jnp.max(s, axis=-1, keepdims=True)
        p = jnp.exp2(s - m).astype(BF16)
        pv = jnp.dot(p, jnp.concatenate([vv, ones_v], axis=1), preferred_element_type=F32)
        acc_o[d][rows_out, :] = jnp.where(lo, pv[:B, :LANES], pv[B:, :LANES])
        acc_d[d][rows_out, :] = jnp.where(lo, pv[:B, LANES:], pv[B:, LANES:])
        acc_m[d][rows_out, :] = jnp.where(lo, m[:B], m[B:])

    for d in ATT_DILATIONS:
        n = T // d
        for r in range(d):
            for jb in range(n // B):
                q_src, q_row = (q_ref, jb * B) if d == 1 else (qd[d], r * n + jb * B)
                rows_out = pl.ds(r + d * B * jb, B, stride=d) if d > 1 else pl.ds(jb * B, B)
                block(q_src, q_row, d, r * (n + B) + jb * B, bias_first if jb == 0 else bias_band,
                      rows_out)

    for c in range(T // ATT_COMBINE_ROWS):
        rows = slice(c * ATT_COMBINE_ROWS, (c + 1) * ATT_COMBINE_ROWS)
        ms = [acc_m[d][rows, :] for d in ATT_DILATIONS]
        mx = functools.reduce(jnp.maximum, ms)
        es = [jnp.exp2(m - mx) for m in ms]
        num = sum(e * acc_o[d][rows, :] for e, d in zip(es, ATT_DILATIONS))
        den = sum(e * acc_d[d][rows, :] for e, d in zip(es, ATT_DILATIONS))
        o_ref[rows, :] = (num / den).astype(o_ref.dtype)

    for d in ATT_DILATIONS:
        n = T // d
        for r in range(d):
            row = r * (n + B)
            kc[d][row:row + B, :] = kc[d][row + n:row + n + B, :]
            vc[d][row:row + B, :] = vc[d][row + n:row + n + B, :]


def _attn(q, k, v, batch, seq):
    T = q.shape[0]
    tiles = seq // ATT_SUPER
    spec = pl.BlockSpec((ATT_SUPER, LANES), lambda b, h, t: (b * tiles + t, h))
    rows = lambda n, dt: pltpu.VMEM((n, LANES), dt)
    kv_rows = [ATT_SUPER + d * ATT_BLOCK for d in ATT_DILATIONS]
    scratch = ([rows(ATT_SUPER, F32)] + [rows(ATT_SUPER, BF16) for _ in ATT_DILATIONS[1:]]
               + [rows(n, BF16) for n in kv_rows] + [rows(n, BF16) for n in kv_rows]
               + [rows(ATT_SUPER, F32) for _ in range(3 * len(ATT_DILATIONS))])
    return pl.pallas_call(
        _attn_kernel, grid=(batch, ATT_WIDTH // LANES, tiles), in_specs=[spec, spec, spec],
        out_specs=spec, out_shape=jax.ShapeDtypeStruct((T, ATT_WIDTH), BF16), scratch_shapes=scratch,
        compiler_params=pltpu.CompilerParams(
            dimension_semantics=("arbitrary", "arbitrary", "arbitrary"),
            vmem_limit_bytes=VMEM_LIMIT_BYTES),
        name="attn")(q, k, v)


def _out_ffn_kernel(x1_ref, og_ref, oa_ref, ga_ref, wo_ref, g2_ref, w1_ref, w3_ref, w2_ref, gf_ref,
                    out_ref):
    oa = _rms(oa_ref[...].astype(F32), ga_ref[...]).astype(BF16)
    x2 = (x1_ref[...]
          + jnp.dot(og_ref[...], wo_ref[:GLA_WIDTH, :], preferred_element_type=F32)
          + jnp.dot(oa, wo_ref[GLA_WIDTH:, :], preferred_element_type=F32))
    h = _rms(x2, g2_ref[...]).astype(BF16)
    x3 = x2 + 0.5 * _swiglu_half_step(h, w1_ref, w3_ref, w2_ref)
    out_ref[...] = _rms(x3, gf_ref[...])


def _out_ffn(x1, og, oa, ga, wo, g2, w1, w3, w2, gf):
    T = x1.shape[0]
    tm = TOKEN_TILE
    row = lambda n: pl.BlockSpec((tm, n), lambda i: (i, 0))
    return pl.pallas_call(
        _out_ffn_kernel, grid=(T // tm,),
        in_specs=[row(D_MODEL), row(GLA_WIDTH), row(ATT_WIDTH), _const_spec(ga.shape),
                  _const_spec(wo.shape), _const_spec(g2.shape),
                  _const_spec(w1.shape), _const_spec(w3.shape), _const_spec(w2.shape),
                  _const_spec(gf.shape)],
        out_specs=row(D_MODEL),
        out_shape=jax.ShapeDtypeStruct((T, D_MODEL), F32),
        compiler_params=pltpu.CompilerParams(dimension_semantics=("arbitrary",),
                                             vmem_limit_bytes=VMEM_LIMIT_BYTES),
        name="out_ffn")(x1, og, oa, ga, wo, g2, w1, w3, w2, gf)


def _rope_tables():
    j = np.arange(LANES) % ATT_HD
    half = ROT_DIM // 2
    inv_freq = ROPE_THETA ** (-np.arange(0, ROT_DIM, 2, dtype=np.float32) / ROT_DIM)
    freq = np.where(j < ROT_DIM, inv_freq[j % half], 0.0).astype(np.float32)
    sgn1 = np.where(j < half, -1.0, 0.0).astype(np.float32)
    sgn2 = np.where((j >= half) & (j < ROT_DIM), 1.0, 0.0).astype(np.float32)
    return freq[None, :], sgn1[None, :], sgn2[None, :]


def _chunk_tril():
    i = np.arange(GLA_HALF)
    same = (i[:, None] // GLA_CHUNK) == (i[None, :] // GLA_CHUNK)
    return jnp.asarray((same & (i[None, :] <= i[:, None])).astype(np.float32), dtype=BF16)


def kernel(x, positions, ffn1_norm, ffn1_w1, ffn1_w3, ffn1_w2, mix_norm, w_in, gla_w_a2, gla_b_a,
           gla_out_norm, att_out_norm, w_out, ffn2_norm, ffn2_w1, ffn2_w3, ffn2_w2, final_norm):
    batch, seq, _ = x.shape
    T = batch * seq
    depth = ffn1_norm.shape[0]
    assert depth == 1, "the final norm is fused into the single layer's last kernel"
    freq, sgn1, sgn2 = (jnp.asarray(t) for t in _rope_tables())
    ltri = _chunk_tril()
    pos = positions.astype(F32).reshape(T, 1)
    xs = x.reshape(T, D_MODEL)

    o_gla_end = 2 * GLA_KW + 2 * GLA_WIDTH
    o_ga_end = o_gla_end + GLA_RANK
    for l in range(depth):
        wi = w_in[l]
        wgla = wi[:, :o_gla_end].astype(BF16)
        wga = jnp.pad(wi[:, o_gla_end:o_ga_end].astype(BF16), ((0, 0), (0, LANES - GLA_RANK)))
        watt = wi[:, o_ga_end:].astype(BF16)
        wa2 = jnp.pad(gla_w_a2[l].astype(BF16), ((0, LANES - GLA_RANK), (0, 0)))
        (x1, o_gla, aq, ak, av), (w1b, w3b, w2b, wob) = _ffn_mix(
            xs, pos, ffn1_norm[l][None, :], ffn1_w1[l].astype(BF16), ffn1_w3[l].astype(BF16),
            ffn1_w2[l].astype(BF16), mix_norm[l][None, :], wgla, wga, wa2, gla_b_a[l][None, :], watt,
            freq, sgn1, sgn2, gla_out_norm[l][None, :], ltri,
            late=(ffn2_w1[l], ffn2_w3[l], ffn2_w2[l], w_out[l]), batch=batch)

        o_att = _attn(aq, ak, av, batch, seq)

        xs = _out_ffn(x1, o_gla, o_att, att_out_norm[l][None, :], wob, ffn2_norm[l][None, :], w1b, w3b,
                      w2b, final_norm[None, :])
    return xs.reshape(batch, seq, D_MODEL)
```

```python
import functools
import math

import jax
import jax.numpy as jnp
import numpy as np
from jax import lax
from jax.experimental import pallas as pl
from jax.experimental.pallas import tpu as pltpu

F32 = jnp.float32
BF16 = jnp.bfloat16

D_MODEL = 1024
D_FF = 2816
GLA_WIDTH = 512
GLA_HEADS = 4
GLA_DV = 128
GLA_DK = 64
GLA_KW = GLA_HEADS * GLA_DK
GLA_RANK = 16
GLA_TAU = 16.0
GLA_CHUNK = 64
ATT_WIDTH = 512
ATT_HEADS = 8
ATT_HD = 64
ROT_DIM = 16
ROPE_THETA = 500000.0
DILATED_PATTERNS = ((128, 1), (512, 4), (2048, 16))
ATT_BLOCK = 128
EPS = 1e-6

LANES = 128
SUBLANES = 8
BF16_SUBLANES = 16
VMEM_LIMIT_BYTES = 56 * 1024 * 1024

TOKEN_TILE = 512
MXU_TILE = 256
FF_SPLITS = (0, (D_FF // MXU_TILE // 2) * MXU_TILE, D_FF)
assert D_FF % MXU_TILE == 0
GLA_TILE = 2048
GLA_HALF = 256
ATT_DILATIONS = tuple(sorted(d for _, d in DILATED_PATTERNS))
ATT_SUPER = max(ATT_DILATIONS) * ATT_BLOCK
ATT_COMBINE_ROWS = 256
NEG = -1e30
LOG2E = math.log2(math.e)

assert all(w // d == ATT_BLOCK for w, d in DILATED_PATTERNS)
assert ATT_DILATIONS[0] == 1 and all(ATT_SUPER % (d * ATT_BLOCK) == 0 for d in ATT_DILATIONS)


def _const_spec(shape):
    nd = len(shape)
    return pl.BlockSpec(shape, lambda *_: (0,) * nd, pipeline_mode=pl.Buffered(1))


def _rms(x, g):
    return x * lax.rsqrt(jnp.mean(x * x, axis=-1, keepdims=True) + EPS) * g


def _swiglu_half_step(h, w1_ref, w3_ref, w2_ref):
    acc = None
    for lo, hi in zip(FF_SPLITS[:-1], FF_SPLITS[1:]):
        sl = slice(lo, hi)
        a = jnp.dot(h, w1_ref[:, sl], preferred_element_type=F32)
        b = jnp.dot(h, w3_ref[:, sl], preferred_element_type=F32)
        g = (a / (1.0 + jnp.exp(-a)) * b).astype(BF16)
        part = jnp.dot(g, w2_ref[sl, :], preferred_element_type=F32)
        acc = part if acc is None else acc + part
    return acc


N_PROJ_OUTPUTS = 9


def _ffn_proj_kernel(x_ref, pos_ref, g1_ref, w1_ref, w3_ref, w2_ref, gmix_ref, wgla_ref, wga_ref,
                     wa2_ref, ba_ref, watt_ref, freq_ref, sgn1_ref, sgn2_ref, *rest):
    n_late = (len(rest) - N_PROJ_OUTPUTS) // 2
    late_in = rest[:n_late]
    x1_ref, gq_ref, gk_ref, gv_ref, gr_ref, la_ref, aq_ref, ak_ref, av_ref = (
        rest[n_late:n_late + N_PROJ_OUTPUTS])
    late_out = rest[n_late + N_PROJ_OUTPUTS:]
    for src, dst in zip(late_in, late_out):
        dst[...] = src[...].astype(BF16)

    x = x_ref[...]
    h = _rms(x, g1_ref[...]).astype(BF16)
    x1 = x + 0.5 * _swiglu_half_step(h, w1_ref, w3_ref, w2_ref)
    x1_ref[...] = x1

    h2 = _rms(x1, gmix_ref[...]).astype(BF16)
    pa = jnp.dot(h2, watt_ref[...], preferred_element_type=F32)
    av_ref[...] = pa[:, 2 * ATT_WIDTH:].astype(BF16)

    pos_col = jnp.broadcast_to(pos_ref[...], (SUBLANES, TOKEN_TILE)).T[:, 0:1]
    ang = pos_col * freq_ref[...]
    cos = jnp.cos(ang)
    sin = jnp.sin(ang)
    s1 = sin * sgn1_ref[...]
    s2 = sin * sgn2_ref[...]
    half = ROT_DIM // 2
    for off, ref, scale in ((0, aq_ref, ATT_HD ** -0.5 * LOG2E), (ATT_WIDTH, ak_ref, 1.0)):
        for cg in range(ATT_WIDTH // LANES):
            t = pa[:, off + cg * LANES: off + (cg + 1) * LANES]
            r = t * cos + pltpu.roll(t, LANES - half, 1) * s1 + pltpu.roll(t, half, 1) * s2
            ref[:, cg * LANES:(cg + 1) * LANES] = (r * scale).astype(BF16)

    ga = jnp.dot(h2, wga_ref[...], preferred_element_type=F32).astype(BF16)
    z = jnp.dot(ga, wa2_ref[...], preferred_element_type=F32) + ba_ref[...]
    la_ref[...] = (jnp.minimum(z, 0.0) - jnp.log(1.0 + jnp.exp(-jnp.abs(z)))) * (1.0 / GLA_TAU)

    pg = jnp.dot(h2, wgla_ref[...], preferred_element_type=F32)
    gq_ref[...] = (pg[:, :GLA_KW] * (GLA_DK ** -0.5)).astype(BF16)
    gk_ref[...] = pg[:, GLA_KW:2 * GLA_KW].astype(BF16)
    gv_ref[...] = pg[:, 2 * GLA_KW:2 * GLA_KW + GLA_WIDTH].astype(BF16)
    gr = pg[:, 2 * GLA_KW + GLA_WIDTH:]
    gr_ref[...] = (gr / (1.0 + jnp.exp(-gr))).astype(BF16)


def _late_chunk_spec(shape, steps):
    rows, cols = shape
    share = 1 if (rows // steps) % BF16_SUBLANES == 0 else 2
    assert rows % (steps // share) == 0 and (rows * share // steps) % BF16_SUBLANES == 0
    return pl.BlockSpec((rows * share // steps, cols), lambda i: (i // share, 0))


def _ffn_proj(x2d, pos, g1, w1, w3, w2, gmix, wgla, wga, wa2, ba, watt, freq, sgn1, sgn2, late):
    T = x2d.shape[0]
    tm = TOKEN_TILE
    steps = T // tm
    row = lambda n: pl.BlockSpec((tm, n), lambda i: (i, 0))
    pos_spec = pl.BlockSpec((None, 1, tm), lambda i: (i, 0, 0))
    late_specs = [_late_chunk_spec(w.shape, steps) for w in late]
    in_specs = [row(D_MODEL), pos_spec, _const_spec(g1.shape), _const_spec(w1.shape), _const_spec(w3.shape),
                _const_spec(w2.shape), _const_spec(gmix.shape), _const_spec(wgla.shape),
                _const_spec(wga.shape), _const_spec(wa2.shape), _const_spec(ba.shape),
                _const_spec(watt.shape), _const_spec(freq.shape), _const_spec(sgn1.shape),
                _const_spec(sgn2.shape)] + late_specs
    out_shape = [jax.ShapeDtypeStruct((T, D_MODEL), F32),
                 jax.ShapeDtypeStruct((T, GLA_KW), BF16), jax.ShapeDtypeStruct((T, GLA_KW), BF16),
                 jax.ShapeDtypeStruct((T, GLA_WIDTH), BF16), jax.ShapeDtypeStruct((T, GLA_WIDTH), BF16),
                 jax.ShapeDtypeStruct((T, GLA_KW), F32),
                 jax.ShapeDtypeStruct((T, ATT_WIDTH), BF16), jax.ShapeDtypeStruct((T, ATT_WIDTH), BF16),
                 jax.ShapeDtypeStruct((T, ATT_WIDTH), BF16)]
    assert len(out_shape) == N_PROJ_OUTPUTS
    out_specs = [row(s.shape[1]) for s in out_shape] + late_specs
    out_shape += [jax.ShapeDtypeStruct(w.shape, BF16) for w in late]
    res = pl.pallas_call(
        _ffn_proj_kernel, grid=(steps,), in_specs=in_specs, out_specs=out_specs, out_shape=out_shape,
        compiler_params=pltpu.CompilerParams(dimension_semantics=("arbitrary",),
                                             vmem_limit_bytes=VMEM_LIMIT_BYTES),
        name="ffn_proj")(x2d, pos, g1, w1, w3, w2, gmix, wgla, wga, wa2, ba, watt, freq, sgn1, sgn2,
                         *late)
    return res[:N_PROJ_OUTPUTS], res[N_PROJ_OUTPUTS:]


def _gla_kernel(q_ref, k_ref, v_ref, la_ref, r_ref, gn_ref, ltri_ref, o_ref, s_ref):
    @pl.when(pl.program_id(1) == 0)
    def _():
        s_ref[...] = jnp.zeros_like(s_ref)

    C = GLA_CHUNK
    r_kk = lax.broadcasted_iota(jnp.int32, (GLA_KW, GLA_KW), 0) // C
    c_kk = lax.broadcasted_iota(jnp.int32, (GLA_KW, GLA_KW), 1) // C
    bd_k = r_kk == c_kk
    r_kv = lax.broadcasted_iota(jnp.int32, (GLA_KW, GLA_WIDTH), 0) // C
    c_kv = lax.broadcasted_iota(jnp.int32, (GLA_KW, GLA_WIDTH), 1) // GLA_DV
    bd_v = r_kv == c_kv
    causal = (lax.broadcasted_iota(jnp.int32, (C, GLA_KW), 1) % C
              <= lax.broadcasted_iota(jnp.int32, (C, GLA_KW), 0))
    ltri = ltri_ref[...]

    for hf in range(GLA_TILE // GLA_HALF):
        base = hf * GLA_HALF
        g = la_ref[base:base + GLA_HALF, :]
        g_hi = g.astype(BF16)
        g_lo = (g - g_hi.astype(F32)).astype(BF16)
        b = (jnp.dot(ltri, g_hi, preferred_element_type=F32)
             + jnp.dot(ltri, g_lo, preferred_element_type=F32))
        for c in range(GLA_HALF // C):
            lo = base + c * C
            bc = b[c * C:(c + 1) * C, :]
            bl = bc[C - 1:C, :]
            qc = q_ref[lo:lo + C, :].astype(F32)
            kc = k_ref[lo:lo + C, :].astype(F32)
            vc = v_ref[lo:lo + C, :]
            qd = (qc * jnp.exp(bc)).astype(BF16)
            k_inv = kc * jnp.exp(-bc)
            ki = k_inv.astype(BF16)
            kt = k_inv * jnp.exp(bl)
            kbd = jnp.where(bd_k, jnp.tile(ki, (GLA_HEADS, 1)), jnp.zeros((), BF16))
            a = lax.dot_general(qd, kbd, (((1,), (1,)), ((), ())), preferred_element_type=F32)
            a = jnp.where(causal, a, 0.0).astype(BF16)
            vbd = jnp.where(bd_v, jnp.tile(vc, (GLA_HEADS, 1)), jnp.zeros((), BF16))
            s_prev = s_ref[...]
            sbd = jnp.where(bd_v, jnp.tile(s_prev.astype(BF16), (1, GLA_HEADS)), jnp.zeros((), BF16))
            o = (jnp.dot(a, vbd, preferred_element_type=F32)
                 + jnp.dot(qd, sbd, preferred_element_type=F32))

            kt_t = kt.T.astype(BF16)
            dec = jnp.exp(bc.T[:, C - 1:C])
            u = jnp.concatenate(
                [jnp.dot(kt_t[h * GLA_DK:(h + 1) * GLA_DK, :], vc[:, h * GLA_DV:(h + 1) * GLA_DV],
                         preferred_element_type=F32) for h in range(GLA_HEADS)], axis=0)
            s_ref[...] = dec * s_prev + u

            gate = r_ref[lo:lo + C, :].astype(F32)
            for h in range(GLA_HEADS):
                hs = slice(h * GLA_DV, (h + 1) * GLA_DV)
                oh = _rms(o[:, hs], gn_ref[:, hs])
                o_ref[lo:lo + C, hs] = (oh * gate[:, hs]).astype(BF16)


def _gla(gq, gk, gv, la, gr, gn, ltri, batch):
    T = gq.shape[0]
    tiles = T // batch // GLA_TILE
    row = lambda n: pl.BlockSpec((GLA_TILE, n), lambda b, t: (b * tiles + t, 0))
    return pl.pallas_call(
        _gla_kernel, grid=(batch, tiles),
        in_specs=[row(GLA_KW), row(GLA_KW), row(GLA_WIDTH), row(GLA_KW), row(GLA_WIDTH),
                  _const_spec(gn.shape), _const_spec(ltri.shape)],
        out_specs=row(GLA_WIDTH),
        out_shape=jax.ShapeDtypeStruct((T, GLA_WIDTH), BF16),
        scratch_shapes=[pltpu.VMEM((GLA_KW, GLA_DV), F32)],
        compiler_params=pltpu.CompilerParams(dimension_semantics=("arbitrary", "arbitrary"),
                                             vmem_limit_bytes=VMEM_LIMIT_BYTES),
        name="gla")(gq, gk, gv, la, gr, gn, ltri)


def _attn_kernel(q_ref, k_ref, v_ref, o_ref, xf_ref, *scratch):
    B = ATT_BLOCK
    T = ATT_SUPER
    nd = len(ATT_DILATIONS)
    qd = dict(zip(ATT_DILATIONS[1:], scratch[:nd - 1]))
    kc = dict(zip(ATT_DILATIONS, scratch[nd - 1:2 * nd - 1]))
    vc = dict(zip(ATT_DILATIONS, scratch[2 * nd - 1:3 * nd - 1]))
    acc = scratch[3 * nd - 1:]
    acc_o = dict(zip(ATT_DILATIONS, acc[0::2]))
    acc_l = dict(zip(ATT_DILATIONS, acc[1::2]))
    tile = pl.program_id(2)

    def stage(src_ref, dst, halo):
        xf_ref[...] = src_ref[...].astype(F32)
        for d in ATT_DILATIONS[1:]:
            n = T // d
            for r in range(d):
                lo_row = r * (n + halo) + halo
                dst[d][lo_row:lo_row + n, :] = xf_ref[pl.ds(r, n, stride=d), :].astype(BF16)

    @pl.when(tile == 0)
    def _():
        zeros = jnp.zeros((B, LANES), BF16)
        for d in ATT_DILATIONS:
            for r in range(d):
                row = r * (T // d + B)
                kc[d][row:row + B, :] = zeros
                vc[d][row:row + B, :] = zeros

    stage(q_ref, qd, 0)
    stage(k_ref, kc, B)
    stage(v_ref, vc, B)
    kc[1][B:, :] = k_ref[...]
    vc[1][B:, :] = v_ref[...]

    qi = lax.broadcasted_iota(jnp.int32, (B, 2 * B), 0)
    ki = lax.broadcasted_iota(jnp.int32, (B, 2 * B), 1)
    band = (ki >= qi) & (ki <= qi + B)
    bias_band = jnp.where(band, 0.0, NEG).astype(F32)
    bias_first = jnp.where(band & ((ki >= B) | (tile > 0)), 0.0, NEG).astype(F32)
    lane = lax.broadcasted_iota(jnp.int32, (B, LANES), 1)
    lo = lane < ATT_HD
    zero_bf = jnp.zeros((), BF16)
    ones_v = jnp.ones((2 * B, LANES), BF16)

    def block(q_src, q_row, d, k_row, bias, rows_out):
        qp = q_src[q_row:q_row + B, :]
        kk = kc[d][k_row:k_row + 2 * B, :]
        vv = vc[d][k_row:k_row + 2 * B, :]
        qs = jnp.concatenate([jnp.where(lo, qp, zero_bf), jnp.where(lo, zero_bf, qp)], axis=0)
        s = lax.dot_general(qs, kk, (((1,), (1,)), ((), ())), preferred_element_type=F32)
        s = s + jnp.concatenate([bias, bias], axis=0)
        m = jnp.max(s, axis=-1, keepdims=True)
        p = jnp.exp2(s - m).astype(BF16)
        pv = jnp.dot(p, jnp.concatenate([vv, ones_v], axis=1), preferred_element_type=F32)
        den = jnp.where(lo, pv[:B, LANES:], pv[B:, LANES:])
        acc_o[d][rows_out, :] = jnp.where(lo, pv[:B, :LANES], pv[B:, :LANES]) / den
        acc_l[d][rows_out, :] = jnp.where(lo, m[:B], m[B:]) + jnp.log2(den)

    for d in ATT_DILATIONS:
        n = T // d
        for r in range(d):
            for jb in range(n // B):
                q_src, q_row = (q_ref, jb * B) if d == 1 else (qd[d], r * n + jb * B)
                rows_out = pl.ds(r + d * B * jb, B, stride=d) if d > 1 else pl.ds(jb * B, B)
                block(q_src, q_row, d, r * (n + B) + jb * B, bias_first if jb == 0 else bias_band,
                      rows_out)

    for c in range(T // ATT_COMBINE_ROWS):
        rows = slice(c * ATT_COMBINE_ROWS, (c + 1) * ATT_COMBINE_ROWS)
        ls = [acc_l[d][rows, :] for d in ATT_DILATIONS]
        mx = functools.reduce(jnp.maximum, ls)
        es = [jnp.exp2(l - mx) for l in ls]
        num = functools.reduce(jnp.add, [e * acc_o[d][rows, :] for e, d in zip(es, ATT_DILATIONS)])
        o_ref[rows, :] = (num / functools.reduce(jnp.add, es)).astype(o_ref.dtype)

    for d in ATT_DILATIONS:
        n = T // d
        for r in range(d):
            row = r * (n + B)
            kc[d][row:row + B, :] = kc[d][row + n:row + n + B, :]
            vc[d][row:row + B, :] = vc[d][row + n:row + n + B, :]


def _attn(q, k, v, batch, seq):
    T = q.shape[0]
    tiles = seq // ATT_SUPER
    spec = pl.BlockSpec((ATT_SUPER, LANES), lambda b, h, t: (b * tiles + t, h))
    rows = lambda n, dt: pltpu.VMEM((n, LANES), dt)
    kv_rows = [ATT_SUPER + d * ATT_BLOCK for d in ATT_DILATIONS]
    scratch = ([rows(ATT_SUPER, F32)] + [rows(ATT_SUPER, BF16) for _ in ATT_DILATIONS[1:]]
               + [rows(n, BF16) for n in kv_rows] + [rows(n, BF16) for n in kv_rows]
               + [rows(ATT_SUPER, F32) for _ in range(2 * len(ATT_DILATIONS))])
    return pl.pallas_call(
        _attn_kernel, grid=(batch, ATT_WIDTH // LANES, tiles), in_specs=[spec, spec, spec],
        out_specs=spec, out_shape=jax.ShapeDtypeStruct((T, ATT_WIDTH), BF16), scratch_shapes=scratch,
        compiler_params=pltpu.CompilerParams(
            dimension_semantics=("arbitrary", "arbitrary", "arbitrary"),
            vmem_limit_bytes=VMEM_LIMIT_BYTES),
        name="attn")(q, k, v)


def _out_ffn_kernel(x1_ref, og_ref, oa_ref, ga_ref, wo_ref, g2_ref, w1_ref, w3_ref, w2_ref, gf_ref,
                    out_ref):
    oa = _rms(oa_ref[...].astype(F32), ga_ref[...]).astype(BF16)
    x2 = (x1_ref[...]
          + jnp.dot(og_ref[...], wo_ref[:GLA_WIDTH, :], preferred_element_type=F32)
          + jnp.dot(oa, wo_ref[GLA_WIDTH:, :], preferred_element_type=F32))
    h = _rms(x2, g2_ref[...]).astype(BF16)
    x3 = x2 + 0.5 * _swiglu_half_step(h, w1_ref, w3_ref, w2_ref)
    out_ref[...] = _rms(x3, gf_ref[...])


def _out_ffn(x1, og, oa, ga, wo, g2, w1, w3, w2, gf):
    T = x1.shape[0]
    tm = TOKEN_TILE
    row = lambda n: pl.BlockSpec((tm, n), lambda i: (i, 0))
    return pl.pallas_call(
        _out_ffn_kernel, grid=(T // tm,),
        in_specs=[row(D_MODEL), row(GLA_WIDTH), row(ATT_WIDTH), _const_spec(ga.shape),
                  _const_spec(wo.shape), _const_spec(g2.shape),
                  _const_spec(w1.shape), _const_spec(w3.shape), _const_spec(w2.shape),
                  _const_spec(gf.shape)],
        out_specs=row(D_MODEL),
        out_shape=jax.ShapeDtypeStruct((T, D_MODEL), F32),
        compiler_params=pltpu.CompilerParams(dimension_semantics=("arbitrary",),
                                             vmem_limit_bytes=VMEM_LIMIT_BYTES),
        name="out_ffn")(x1, og, oa, ga, wo, g2, w1, w3, w2, gf)


def _rope_tables():
    j = np.arange(LANES) % ATT_HD
    half = ROT_DIM // 2
    inv_freq = ROPE_THETA ** (-np.arange(0, ROT_DIM, 2, dtype=np.float32) / ROT_DIM)
    freq = np.where(j < ROT_DIM, inv_freq[j % half], 0.0).astype(np.float32)
    sgn1 = np.where(j < half, -1.0, 0.0).astype(np.float32)
    sgn2 = np.where((j >= half) & (j < ROT_DIM), 1.0, 0.0).astype(np.float32)
    return freq[None, :], sgn1[None, :], sgn2[None, :]


def _chunk_tril():
    i = np.arange(GLA_HALF)
    same = (i[:, None] // GLA_CHUNK) == (i[None, :] // GLA_CHUNK)
    return jnp.asarray((same & (i[None, :] <= i[:, None])).astype(np.float32), dtype=BF16)


def kernel(x, positions, ffn1_norm, ffn1_w1, ffn1_w3, ffn1_w2, mix_norm, w_in, gla_w_a2, gla_b_a,
           gla_out_norm, att_out_norm, w_out, ffn2_norm, ffn2_w1, ffn2_w3, ffn2_w2, final_norm):
    batch, seq, _ = x.shape
    T = batch * seq
    depth = ffn1_norm.shape[0]
    assert depth == 1, "the final norm is fused into the single layer's last kernel"
    freq, sgn1, sgn2 = (jnp.asarray(t) for t in _rope_tables())
    ltri = _chunk_tril()
    pos = positions.astype(F32).reshape(T // TOKEN_TILE, 1, TOKEN_TILE)
    xs = x.reshape(T, D_MODEL)

    o_gla_end = 2 * GLA_KW + 2 * GLA_WIDTH
    o_ga_end = o_gla_end + GLA_RANK
    for l in range(depth):
        wi = w_in[l]
        wgla = wi[:, :o_gla_end].astype(BF16)
        wga = jnp.pad(wi[:, o_gla_end:o_ga_end].astype(BF16), ((0, 0), (0, LANES - GLA_RANK)))
        watt = wi[:, o_ga_end:].astype(BF16)
        wa2 = jnp.pad(gla_w_a2[l].astype(BF16), ((0, LANES - GLA_RANK), (0, 0)))
        (x1, gq, gk, gv, gr, la, aq, ak, av), (w1b, w3b, w2b, wob) = _ffn_proj(
            xs, pos, ffn1_norm[l][None, :], ffn1_w1[l].astype(BF16), ffn1_w3[l].astype(BF16),
            ffn1_w2[l].astype(BF16), mix_norm[l][None, :], wgla, wga, wa2, gla_b_a[l][None, :], watt,
            freq, sgn1, sgn2, late=(ffn2_w1[l], ffn2_w3[l], ffn2_w2[l], w_out[l]))

        o_gla = _gla(gq, gk, gv, la, gr, gla_out_norm[l][None, :], ltri, batch)

        o_att = _attn(aq, ak, av, batch, seq)

        xs = _out_ffn(x1, o_gla, o_att, att_out_norm[l][None, :], wob, ffn2_norm[l][None, :], w1b, w3b,
                      w2b, final_norm[None, :])
    return xs.reshape(batch, seq, D_MODEL)
```

```python
import functools
import math

import jax
import jax.numpy as jnp
import numpy as np
from jax import lax
from jax.experimental import pallas as pl
from jax.experimental.pallas import tpu as pltpu

F32 = jnp.float32
BF16 = jnp.bfloat16

D_MODEL = 1024
D_FF = 2816
GLA_WIDTH = 512
GLA_HEADS = 4
GLA_DV = 128
GLA_DK = 64
GLA_KW = GLA_HEADS * GLA_DK
GLA_RANK = 16
GLA_TAU = 16.0
GLA_CHUNK = 64
ATT_WIDTH = 512
ATT_HEADS = 8
ATT_HD = 64
ROT_DIM = 16
ROPE_THETA = 500000.0
DILATED_PATTERNS = ((128, 1), (512, 4), (2048, 16))
ATT_BLOCK = 128
EPS = 1e-6

LANES = 128
ATT_PAIRS = ATT_WIDTH // LANES
SUBLANES = 8
BF16_SUBLANES = 16
VMEM_LIMIT_BYTES = 56 * 1024 * 1024

TOKEN_TILE = 512
MXU_TILE = 256
FF_SPLITS = (0, (D_FF // MXU_TILE // 2) * MXU_TILE, D_FF)
assert D_FF % MXU_TILE == 0
GLA_TILE = 2048
GLA_HALF = 256
ATT_DILATIONS = tuple(sorted(d for _, d in DILATED_PATTERNS))
ATT_SUPER = max(ATT_DILATIONS) * ATT_BLOCK
ATT_COMBINE_ROWS = 256
NEG = -1e30
LOG2E = math.log2(math.e)

assert all(w // d == ATT_BLOCK for w, d in DILATED_PATTERNS)
assert ATT_DILATIONS[0] == 1 and all(ATT_SUPER % (d * ATT_BLOCK) == 0 for d in ATT_DILATIONS)


def _const_spec(shape):
    nd = len(shape)
    return pl.BlockSpec(shape, lambda *_: (0,) * nd, pipeline_mode=pl.Buffered(1))


def _rms(x, g):
    return x * lax.rsqrt(jnp.mean(x * x, axis=-1, keepdims=True) + EPS) * g


def _swiglu_half_step(h, w1_ref, w3_ref, w2_ref):
    acc = None
    for lo, hi in zip(FF_SPLITS[:-1], FF_SPLITS[1:]):
        sl = slice(lo, hi)
        a = jnp.dot(h, w1_ref[:, sl], preferred_element_type=F32)
        b = jnp.dot(h, w3_ref[:, sl], preferred_element_type=F32)
        g = (a / (1.0 + jnp.exp(-a)) * b).astype(BF16)
        part = jnp.dot(g, w2_ref[sl, :], preferred_element_type=F32)
        acc = part if acc is None else acc + part
    return acc


N_PROJ_OUTPUTS = 9


def _ffn_proj_kernel(x_ref, pos_ref, g1_ref, w1_ref, w3_ref, w2_ref, gmix_ref, wgla_ref, wga_ref,
                     wa2_ref, ba_ref, watt_ref, freq_ref, sgn1_ref, sgn2_ref, *rest):
    n_late = (len(rest) - N_PROJ_OUTPUTS) // 2
    late_in = rest[:n_late]
    x1_ref, gq_ref, gk_ref, gv_ref, gr_ref, la_ref, aq_ref, ak_ref, av_ref = (
        rest[n_late:n_late + N_PROJ_OUTPUTS])
    late_out = rest[n_late + N_PROJ_OUTPUTS:]
    for src, dst in zip(late_in, late_out):
        dst[...] = src[...].astype(BF16)

    x = x_ref[...]
    h = _rms(x, g1_ref[...]).astype(BF16)
    x1 = x + 0.5 * _swiglu_half_step(h, w1_ref, w3_ref, w2_ref)
    x1_ref[...] = x1

    h2 = _rms(x1, gmix_ref[...]).astype(BF16)
    pa = jnp.dot(h2, watt_ref[...], preferred_element_type=F32)
    for cg in range(ATT_PAIRS):
        av_ref[cg] = pa[:, 2 * ATT_WIDTH + cg * LANES:2 * ATT_WIDTH + (cg + 1) * LANES].astype(BF16)

    pos_col = jnp.broadcast_to(pos_ref[...], (SUBLANES, TOKEN_TILE)).T[:, 0:1]
    ang = pos_col * freq_ref[...]
    cos = jnp.cos(ang)
    sin = jnp.sin(ang)
    s1 = sin * sgn1_ref[...]
    s2 = sin * sgn2_ref[...]
    half = ROT_DIM // 2
    for off, ref, scale in ((0, aq_ref, ATT_HD ** -0.5 * LOG2E), (ATT_WIDTH, ak_ref, 1.0)):
        for cg in range(ATT_PAIRS):
            t = pa[:, off + cg * LANES: off + (cg + 1) * LANES]
            r = t * cos + pltpu.roll(t, LANES - half, 1) * s1 + pltpu.roll(t, half, 1) * s2
            ref[cg] = (r * scale).astype(BF16)

    ga = jnp.dot(h2, wga_ref[...], preferred_element_type=F32).astype(BF16)
    z = jnp.dot(ga, wa2_ref[...], preferred_element_type=F32) + ba_ref[...]
    la_ref[...] = (jnp.minimum(z, 0.0) - jnp.log(1.0 + jnp.exp(-jnp.abs(z)))) * (1.0 / GLA_TAU)

    pg = jnp.dot(h2, wgla_ref[...], preferred_element_type=F32)
    gq_ref[...] = (pg[:, :GLA_KW] * (GLA_DK ** -0.5)).astype(BF16)
    gk_ref[...] = pg[:, GLA_KW:2 * GLA_KW].astype(BF16)
    gv_ref[...] = pg[:, 2 * GLA_KW:2 * GLA_KW + GLA_WIDTH].astype(BF16)
    gr = pg[:, 2 * GLA_KW + GLA_WIDTH:]
    gr_ref[...] = (gr / (1.0 + jnp.exp(-gr))).astype(BF16)


def _late_chunk_spec(shape, steps):
    rows, cols = shape
    share = 1 if (rows // steps) % BF16_SUBLANES == 0 else 2
    assert rows % (steps // share) == 0 and (rows * share // steps) % BF16_SUBLANES == 0
    return pl.BlockSpec((rows * share // steps, cols), lambda i: (i // share, 0))


def _ffn_proj(x2d, pos, g1, w1, w3, w2, gmix, wgla, wga, wa2, ba, watt, freq, sgn1, sgn2, late):
    T = x2d.shape[0]
    tm = TOKEN_TILE
    steps = T // tm
    row = lambda n: pl.BlockSpec((tm, n), lambda i: (i, 0))
    pos_spec = pl.BlockSpec((None, 1, tm), lambda i: (i, 0, 0))
    late_specs = [_late_chunk_spec(w.shape, steps) for w in late]
    in_specs = [row(D_MODEL), pos_spec, _const_spec(g1.shape), _const_spec(w1.shape), _const_spec(w3.shape),
                _const_spec(w2.shape), _const_spec(gmix.shape), _const_spec(wgla.shape),
                _const_spec(wga.shape), _const_spec(wa2.shape), _const_spec(ba.shape),
                _const_spec(watt.shape), _const_spec(freq.shape), _const_spec(sgn1.shape),
                _const_spec(sgn2.shape)] + late_specs
    out_shape = [jax.ShapeDtypeStruct((T, D_MODEL), F32),
                 jax.ShapeDtypeStruct((T, GLA_KW), BF16), jax.ShapeDtypeStruct((T, GLA_KW), BF16),
                 jax.ShapeDtypeStruct((T, GLA_WIDTH), BF16), jax.ShapeDtypeStruct((T, GLA_WIDTH), BF16),
                 jax.ShapeDtypeStruct((T, GLA_KW), F32),
                 ] + [jax.ShapeDtypeStruct((ATT_PAIRS, T, LANES), BF16)] * 3
    assert len(out_shape) == N_PROJ_OUTPUTS
    pair_major = pl.BlockSpec((ATT_PAIRS, tm, LANES), lambda i: (0, i, 0))
    out_specs = [row(s.shape[1]) for s in out_shape[:-3]] + [pair_major] * 3 + late_specs
    out_shape += [jax.ShapeDtypeStruct(w.shape, BF16) for w in late]
    res = pl.pallas_call(
        _ffn_proj_kernel, grid=(steps,), in_specs=in_specs, out_specs=out_specs, out_shape=out_shape,
        compiler_params=pltpu.CompilerParams(dimension_semantics=("arbitrary",),
                                             vmem_limit_bytes=VMEM_LIMIT_BYTES),
        name="ffn_proj")(x2d, pos, g1, w1, w3, w2, gmix, wgla, wga, wa2, ba, watt, freq, sgn1, sgn2,
                         *late)
    return res[:N_PROJ_OUTPUTS], res[N_PROJ_OUTPUTS:]


def _gla_kernel(q_ref, k_ref, v_ref, la_ref, r_ref, gn_ref, ltri_ref, o_ref, s_ref):
    @pl.when(pl.program_id(1) == 0)
    def _():
        s_ref[...] = jnp.zeros_like(s_ref)

    C = GLA_CHUNK
    r_kk = lax.broadcasted_iota(jnp.int32, (GLA_KW, GLA_KW), 0) // C
    c_kk = lax.broadcasted_iota(jnp.int32, (GLA_KW, GLA_KW), 1) // C
    bd_k = r_kk == c_kk
    r_kv = lax.broadcasted_iota(jnp.int32, (GLA_KW, GLA_WIDTH), 0) // C
    c_kv = lax.broadcasted_iota(jnp.int32, (GLA_KW, GLA_WIDTH), 1) // GLA_DV
    bd_v = r_kv == c_kv
    causal = (lax.broadcasted_iota(jnp.int32, (C, GLA_KW), 1) % C
              <= lax.broadcasted_iota(jnp.int32, (C, GLA_KW), 0))
    ltri = ltri_ref[...]

    for hf in range(GLA_TILE // GLA_HALF):
        base = hf * GLA_HALF
        g = la_ref[base:base + GLA_HALF, :]
        g_hi = g.astype(BF16)
        g_lo = (g - g_hi.astype(F32)).astype(BF16)
        b = (jnp.dot(ltri, g_hi, preferred_element_type=F32)
             + jnp.dot(ltri, g_lo, preferred_element_type=F32))
        for c in range(GLA_HALF // C):
            lo = base + c * C
            bc = b[c * C:(c + 1) * C, :]
            bl = bc[C - 1:C, :]
            qc = q_ref[lo:lo + C, :].astype(F32)
            kc = k_ref[lo:lo + C, :].astype(F32)
            vc = v_ref[lo:lo + C, :]
            qd = (qc * jnp.exp(bc)).astype(BF16)
            k_inv = kc * jnp.exp(-bc)
            ki = k_inv.astype(BF16)
            kt = k_inv * jnp.exp(bl)
            kbd = jnp.where(bd_k, jnp.tile(ki, (GLA_HEADS, 1)), jnp.zeros((), BF16))
            a = lax.dot_general(qd, kbd, (((1,), (1,)), ((), ())), preferred_element_type=F32)
            a = jnp.where(causal, a, 0.0).astype(BF16)
            vbd = jnp.where(bd_v, jnp.tile(vc, (GLA_HEADS, 1)), jnp.zeros((), BF16))
            s_prev = s_ref[...]
            sbd = jnp.where(bd_v, jnp.tile(s_prev.astype(BF16), (1, GLA_HEADS)), jnp.zeros((), BF16))
            o = (jnp.dot(a, vbd, preferred_element_type=F32)
                 + jnp.dot(qd, sbd, preferred_element_type=F32))

            kt_t = kt.T.astype(BF16)
            dec = jnp.exp(bc.T[:, C - 1:C])
            u = jnp.concatenate(
                [jnp.dot(kt_t[h * GLA_DK:(h + 1) * GLA_DK, :], vc[:, h * GLA_DV:(h + 1) * GLA_DV],
                         preferred_element_type=F32) for h in range(GLA_HEADS)], axis=0)
            s_ref[...] = dec * s_prev + u

            gate = r_ref[lo:lo + C, :].astype(F32)
            for h in range(GLA_HEADS):
                hs = slice(h * GLA_DV, (h + 1) * GLA_DV)
                oh = _rms(o[:, hs], gn_ref[:, hs])
                o_ref[lo:lo + C, hs] = (oh * gate[:, hs]).astype(BF16)


def _gla(gq, gk, gv, la, gr, gn, ltri, batch):
    T = gq.shape[0]
    tiles = T // batch // GLA_TILE
    row = lambda n: pl.BlockSpec((GLA_TILE, n), lambda b, t: (b * tiles + t, 0))
    return pl.pallas_call(
        _gla_kernel, grid=(batch, tiles),
        in_specs=[row(GLA_KW), row(GLA_KW), row(GLA_WIDTH), row(GLA_KW), row(GLA_WIDTH),
                  _const_spec(gn.shape), _const_spec(ltri.shape)],
        out_specs=row(GLA_WIDTH),
        out_shape=jax.ShapeDtypeStruct((T, GLA_WIDTH), BF16),
        scratch_shapes=[pltpu.VMEM((GLA_KW, GLA_DV), F32)],
        compiler_params=pltpu.CompilerParams(dimension_semantics=("arbitrary", "arbitrary"),
                                             vmem_limit_bytes=VMEM_LIMIT_BYTES),
        name="gla")(gq, gk, gv, la, gr, gn, ltri)


def _attn_kernel(q_ref, k_ref, v_ref, o_ref, xf_ref, *scratch):
    B = ATT_BLOCK
    T = ATT_SUPER
    nd = len(ATT_DILATIONS)
    qd = dict(zip(ATT_DILATIONS[1:], scratch[:nd - 1]))
    kc = dict(zip(ATT_DILATIONS, scratch[nd - 1:2 * nd - 1]))
    vc = dict(zip(ATT_DILATIONS, scratch[2 * nd - 1:3 * nd - 1]))
    acc = scratch[3 * nd - 1:]
    acc_o = dict(zip(ATT_DILATIONS, acc[0::2]))
    acc_l = dict(zip(ATT_DILATIONS, acc[1::2]))
    tile = pl.program_id(2)

    def stage(src_ref, dst, halo):
        xf_ref[...] = src_ref[...].astype(F32)
        for d in ATT_DILATIONS[1:]:
            n = T // d
            for r in range(d):
                lo_row = r * (n + halo) + halo
                dst[d][lo_row:lo_row + n, :] = xf_ref[pl.ds(r, n, stride=d), :].astype(BF16)

    @pl.when(tile == 0)
    def _():
        zeros = jnp.zeros((B, LANES), BF16)
        for d in ATT_DILATIONS:
            for r in range(d):
                row = r * (T // d + B)
                kc[d][row:row + B, :] = zeros
                vc[d][row:row + B, :] = zeros

    stage(q_ref, qd, 0)
    stage(k_ref, kc, B)
    stage(v_ref, vc, B)
    kc[1][B:, :] = k_ref[...]
    vc[1][B:, :] = v_ref[...]

    qi = lax.broadcasted_iota(jnp.int32, (B, 2 * B), 0)
    ki = lax.broadcasted_iota(jnp.int32, (B, 2 * B), 1)
    band = (ki >= qi) & (ki <= qi + B)
    bias_band = jnp.where(band, 0.0, NEG).astype(F32)
    bias_first = jnp.where(band & ((ki >= B) | (tile > 0)), 0.0, NEG).astype(F32)
    lane = lax.broadcasted_iota(jnp.int32, (B, LANES), 1)
    lo = lane < ATT_HD
    zero_bf = jnp.zeros((), BF16)
    ones_v = jnp.ones((2 * B, LANES), BF16)

    def block(q_src, q_row, d, k_row, bias, rows_out):
        qp = q_src[q_row:q_row + B, :]
        kk = kc[d][k_row:k_row + 2 * B, :]
        vv = vc[d][k_row:k_row + 2 * B, :]
        qs = jnp.concatenate([jnp.where(lo, qp, zero_bf), jnp.where(lo, zero_bf, qp)], axis=0)
        s = lax.dot_general(qs, kk, (((1,), (1,)), ((), ())), preferred_element_type=F32)
        s = s + jnp.concatenate([bias, bias], axis=0)
        m = jnp.max(s, axis=-1, keepdims=True)
        p = jnp.exp2(s - m).astype(BF16)
        pv = jnp.dot(p, jnp.concatenate([vv, ones_v], axis=1), preferred_element_type=F32)
        den = jnp.where(lo, pv[:B, LANES:], pv[B:, LANES:])
        acc_o[d][rows_out, :] = jnp.where(lo, pv[:B, :LANES], pv[B:, :LANES]) / den
        acc_l[d][rows_out, :] = jnp.where(lo, m[:B], m[B:]) + jnp.log2(den)

    for d in ATT_DILATIONS:
        n = T // d
        for r in range(d):
            for jb in range(n // B):
                q_src, q_row = (q_ref, jb * B) if d == 1 else (qd[d], r * n + jb * B)
                rows_out = pl.ds(r + d * B * jb, B, stride=d) if d > 1 else pl.ds(jb * B, B)
                block(q_src, q_row, d, r * (n + B) + jb * B, bias_first if jb == 0 else bias_band,
                      rows_out)

    for c in range(T // ATT_COMBINE_ROWS):
        rows = slice(c * ATT_COMBINE_ROWS, (c + 1) * ATT_COMBINE_ROWS)
        ls = [acc_l[d][rows, :] for d in ATT_DILATIONS]
        mx = functools.reduce(jnp.maximum, ls)
        es = [jnp.exp2(l - mx) for l in ls]
        num = functools.reduce(jnp.add, [e * acc_o[d][rows, :] for e, d in zip(es, ATT_DILATIONS)])
        o_ref[rows, :] = (num / functools.reduce(jnp.add, es)).astype(o_ref.dtype)

    for d in ATT_DILATIONS:
        n = T // d
        for r in range(d):
            row = r * (n + B)
            kc[d][row:row + B, :] = kc[d][row + n:row + n + B, :]
            vc[d][row:row + B, :] = vc[d][row + n:row + n + B, :]


def _attn(q, k, v, batch, seq):
    T = q.shape[1]
    tiles = seq // ATT_SUPER
    spec = pl.BlockSpec((None, ATT_SUPER, LANES), lambda b, h, t: (h, b * tiles + t, 0))
    rows = lambda n, dt: pltpu.VMEM((n, LANES), dt)
    kv_rows = [ATT_SUPER + d * ATT_BLOCK for d in ATT_DILATIONS]
    scratch = ([rows(ATT_SUPER, F32)] + [rows(ATT_SUPER, BF16) for _ in ATT_DILATIONS[1:]]
               + [rows(n, BF16) for n in kv_rows] + [rows(n, BF16) for n in kv_rows]
               + [rows(ATT_SUPER, F32) for _ in range(2 * len(ATT_DILATIONS))])
    return pl.pallas_call(
        _attn_kernel, grid=(batch, ATT_PAIRS, tiles), in_specs=[spec, spec, spec],
        out_specs=spec, out_shape=jax.ShapeDtypeStruct((ATT_PAIRS, T, LANES), BF16), scratch_shapes=scratch,
        compiler_params=pltpu.CompilerParams(
            dimension_semantics=("arbitrary", "arbitrary", "arbitrary"),
            vmem_limit_bytes=VMEM_LIMIT_BYTES),
        name="attn")(q, k, v)


def _out_ffn_kernel(x1_ref, og_ref, oa_ref, ga_ref, wo_ref, g2_ref, w1_ref, w3_ref, w2_ref, gf_ref,
                    out_ref):
    oa = jnp.concatenate([oa_ref[c] for c in range(ATT_PAIRS)], axis=1)
    oa = _rms(oa.astype(F32), ga_ref[...]).astype(BF16)
    x2 = (x1_ref[...]
          + jnp.dot(og_ref[...], wo_ref[:GLA_WIDTH, :], preferred_element_type=F32)
          + jnp.dot(oa, wo_ref[GLA_WIDTH:, :], preferred_element_type=F32))
    h = _rms(x2, g2_ref[...]).astype(BF16)
    x3 = x2 + 0.5 * _swiglu_half_step(h, w1_ref, w3_ref, w2_ref)
    out_ref[...] = _rms(x3, gf_ref[...])


def _out_ffn(x1, og, oa, ga, wo, g2, w1, w3, w2, gf):
    T = x1.shape[0]
    tm = TOKEN_TILE
    row = lambda n: pl.BlockSpec((tm, n), lambda i: (i, 0))
    return pl.pallas_call(
        _out_ffn_kernel, grid=(T // tm,),
        in_specs=[row(D_MODEL), row(GLA_WIDTH),
                  pl.BlockSpec((ATT_PAIRS, tm, LANES), lambda i: (0, i, 0)), _const_spec(ga.shape),
                  _const_spec(wo.shape), _const_spec(g2.shape),
                  _const_spec(w1.shape), _const_spec(w3.shape), _const_spec(w2.shape),
                  _const_spec(gf.shape)],
        out_specs=row(D_MODEL),
        out_shape=jax.ShapeDtypeStruct((T, D_MODEL), F32),
        compiler_params=pltpu.CompilerParams(dimension_semantics=("arbitrary",),
                                             vmem_limit_bytes=VMEM_LIMIT_BYTES),
        name="out_ffn")(x1, og, oa, ga, wo, g2, w1, w3, w2, gf)


def _rope_tables():
    j = np.arange(LANES) % ATT_HD
    half = ROT_DIM // 2
    inv_freq = ROPE_THETA ** (-np.arange(0, ROT_DIM, 2, dtype=np.float32) / ROT_DIM)
    freq = np.where(j < ROT_DIM, inv_freq[j % half], 0.0).astype(np.float32)
    sgn1 = np.where(j < half, -1.0, 0.0).astype(np.float32)
    sgn2 = np.where((j >= half) & (j < ROT_DIM), 1.0, 0.0).astype(np.float32)
    return freq[None, :], sgn1[None, :], sgn2[None, :]


def _chunk_tril():
    i = np.arange(GLA_HALF)
    same = (i[:, None] // GLA_CHUNK) == (i[None, :] // GLA_CHUNK)
    return jnp.asarray((same & (i[None, :] <= i[:, None])).astype(np.float32), dtype=BF16)


def kernel(x, positions, ffn1_norm, ffn1_w1, ffn1_w3, ffn1_w2, mix_norm, w_in, gla_w_a2, gla_b_a,
           gla_out_norm, att_out_norm, w_out, ffn2_norm, ffn2_w1, ffn2_w3, ffn2_w2, final_norm):
    batch, seq, _ = x.shape
    T = batch * seq
    depth = ffn1_norm.shape[0]
    assert depth == 1, "the final norm is fused into the single layer's last kernel"
    freq, sgn1, sgn2 = (jnp.asarray(t) for t in _rope_tables())
    ltri = _chunk_tril()
    pos = positions.astype(F32).reshape(T // TOKEN_TILE, 1, TOKEN_TILE)
    xs = x.reshape(T, D_MODEL)

    o_gla_end = 2 * GLA_KW + 2 * GLA_WIDTH
    o_ga_end = o_gla_end + GLA_RANK
    for l in range(depth):
        wi = w_in[l]
        wgla = wi[:, :o_gla_end].astype(BF16)
        wga = jnp.pad(wi[:, o_gla_end:o_ga_end].astype(BF16), ((0, 0), (0, LANES - GLA_RANK)))
        watt = wi[:, o_ga_end:].astype(BF16)
        wa2 = jnp.pad(gla_w_a2[l].astype(BF16), ((0, LANES - GLA_RANK), (0, 0)))
        (x1, gq, gk, gv, gr, la, aq, ak, av), (w1b, w3b, w2b, wob) = _ffn_proj(
            xs, pos, ffn1_norm[l][None, :], ffn1_w1[l].astype(BF16), ffn1_w3[l].astype(BF16),
            ffn1_w2[l].astype(BF16), mix_norm[l][None, :], wgla, wga, wa2, gla_b_a[l][None, :], watt,
            freq, sgn1, sgn2, late=(ffn2_w1[l], ffn2_w3[l], ffn2_w2[l], w_out[l]))

        o_gla = _gla(gq, gk, gv, la, gr, gla_out_norm[l][None, :], ltri, batch)

        o_att = _attn(aq, ak, av, batch, seq)

        xs = _out_ffn(x1, o_gla, o_att, att_out_norm[l][None, :], wob, ffn2_norm[l][None, :], w1b, w3b,
                      w2b, final_norm[None, :])
    return xs.reshape(batch, seq, D_MODEL)
```

```python
import functools
import math

import jax
import jax.numpy as jnp
import numpy as np
from jax import lax
from jax.experimental import pallas as pl
from jax.experimental.pallas import tpu as pltpu

F32 = jnp.float32
BF16 = jnp.bfloat16

D_MODEL = 1024
D_FF = 2816
GLA_WIDTH = 512
GLA_HEADS = 4
GLA_DV = 128
GLA_DK = 64
GLA_KW = GLA_HEADS * GLA_DK
GLA_RANK = 16
GLA_TAU = 16.0
GLA_CHUNK = 64
ATT_WIDTH = 512
ATT_HEADS = 8
ATT_HD = 64
ROT_DIM = 16
ROPE_THETA = 500000.0
DILATED_PATTERNS = ((128, 1), (512, 4), (2048, 16))
ATT_BLOCK = 128
EPS = 1e-6

LANES = 128
ATT_PAIRS = ATT_WIDTH // LANES
SUBLANES = 8
BF16_SUBLANES = 16
VMEM_LIMIT_BYTES = 56 * 1024 * 1024

TOKEN_TILE = 512
OUT_TOKEN_TILE = 1024
MXU_TILE = 256
FF_SPLITS = (0, (D_FF // MXU_TILE // 2) * MXU_TILE, D_FF)
assert D_FF % MXU_TILE == 0
GLA_TILE = 2048
GLA_HALF = 256
ATT_DILATIONS = tuple(sorted(d for _, d in DILATED_PATTERNS))
ATT_SUPER = max(ATT_DILATIONS) * ATT_BLOCK
ATT_COMBINE_ROWS = 256
NEG = -1e30
LOG2E = math.log2(math.e)

assert all(w // d == ATT_BLOCK for w, d in DILATED_PATTERNS)
assert ATT_DILATIONS[0] == 1 and all(ATT_SUPER % (d * ATT_BLOCK) == 0 for d in ATT_DILATIONS)


def _const_spec(shape):
    nd = len(shape)
    return pl.BlockSpec(shape, lambda *_: (0,) * nd, pipeline_mode=pl.Buffered(1))


def _rms(x, g):
    return x * lax.rsqrt(jnp.mean(x * x, axis=-1, keepdims=True) + EPS) * g


def _swiglu_half_step(h, w1_ref, w3_ref, w2_ref):
    acc = None
    for lo, hi in zip(FF_SPLITS[:-1], FF_SPLITS[1:]):
        sl = slice(lo, hi)
        a = jnp.dot(h, w1_ref[:, sl], preferred_element_type=F32)
        b = jnp.dot(h, w3_ref[:, sl], preferred_element_type=F32)
        g = (a / (1.0 + jnp.exp(-a)) * b).astype(BF16)
        part = jnp.dot(g, w2_ref[sl, :], preferred_element_type=F32)
        acc = part if acc is None else acc + part
    return acc


N_PROJ_OUTPUTS = 9
W_GLA = 2 * GLA_KW + 2 * GLA_WIDTH
W_ATT = 3 * ATT_WIDTH


def _ffn_proj_kernel(x_ref, pos_ref, g1_ref, w1_ref, w3_ref, w2_ref, gmix_ref, win_ref, wa2_ref, ba_ref,
                     freq_ref, sgn1_ref, sgn2_ref, *rest):
    n_late = (len(rest) - N_PROJ_OUTPUTS) // 2
    late_in = rest[:n_late]
    x1_ref, gq_ref, gk_ref, gv_ref, gr_ref, la_ref, aq_ref, ak_ref, av_ref = (
        rest[n_late:n_late + N_PROJ_OUTPUTS])
    late_out = rest[n_late + N_PROJ_OUTPUTS:]
    for src, dst in zip(late_in, late_out):
        dst[...] = src[...].astype(BF16)

    x = x_ref[...]
    h = _rms(x, g1_ref[...]).astype(BF16)
    x1 = x + 0.5 * _swiglu_half_step(h, w1_ref, w3_ref, w2_ref)
    x1_ref[...] = x1

    h2 = _rms(x1, gmix_ref[...]).astype(BF16)
    pa = jnp.dot(h2, win_ref[:, W_GLA:W_GLA + W_ATT], preferred_element_type=F32)
    for cg in range(ATT_PAIRS):
        av_ref[cg] = pa[:, 2 * ATT_WIDTH + cg * LANES:2 * ATT_WIDTH + (cg + 1) * LANES].astype(BF16)

    pos_col = jnp.broadcast_to(pos_ref[...], (SUBLANES, TOKEN_TILE)).T[:, 0:1]
    ang = pos_col * freq_ref[...]
    cos = jnp.cos(ang)
    sin = jnp.sin(ang)
    s1 = sin * sgn1_ref[...]
    s2 = sin * sgn2_ref[...]
    half = ROT_DIM // 2
    for off, ref, scale in ((0, aq_ref, ATT_HD ** -0.5 * LOG2E), (ATT_WIDTH, ak_ref, 1.0)):
        for cg in range(ATT_PAIRS):
            t = pa[:, off + cg * LANES: off + (cg + 1) * LANES]
            r = t * cos + pltpu.roll(t, LANES - half, 1) * s1 + pltpu.roll(t, half, 1) * s2
            ref[cg] = (r * scale).astype(BF16)

    ga = jnp.dot(h2, win_ref[:, W_GLA + W_ATT:], preferred_element_type=F32).astype(BF16)
    z = jnp.dot(ga, wa2_ref[...], preferred_element_type=F32) + ba_ref[...]
    la_ref[...] = (jnp.minimum(z, 0.0) - jnp.log(1.0 + jnp.exp(-jnp.abs(z)))) * (1.0 / GLA_TAU)

    pg = jnp.dot(h2, win_ref[:, :W_GLA], preferred_element_type=F32)
    gq_ref[...] = (pg[:, :GLA_KW] * (GLA_DK ** -0.5)).astype(BF16)
    gk_ref[...] = pg[:, GLA_KW:2 * GLA_KW].astype(BF16)
    gv_ref[...] = pg[:, 2 * GLA_KW:2 * GLA_KW + GLA_WIDTH].astype(BF16)
    gr = pg[:, 2 * GLA_KW + GLA_WIDTH:]
    gr_ref[...] = (gr / (1.0 + jnp.exp(-gr))).astype(BF16)


def _late_chunk_spec(shape, steps):
    rows, cols = shape
    share = 1 if (rows // steps) % BF16_SUBLANES == 0 else 2
    assert rows % (steps // share) == 0 and (rows * share // steps) % BF16_SUBLANES == 0
    return pl.BlockSpec((rows * share // steps, cols), lambda i: (i // share, 0))


def _ffn_proj(x2d, pos, g1, w1, w3, w2, gmix, win, wa2, ba, freq, sgn1, sgn2, late):
    T = x2d.shape[0]
    tm = TOKEN_TILE
    steps = T // tm
    row = lambda n: pl.BlockSpec((tm, n), lambda i: (i, 0))
    pos_spec = pl.BlockSpec((None, 1, tm), lambda i: (i, 0, 0))
    late_specs = [_late_chunk_spec(w.shape, steps) for w in late]
    in_specs = [row(D_MODEL), pos_spec, _const_spec(g1.shape), _const_spec(w1.shape), _const_spec(w3.shape),
                _const_spec(w2.shape), _const_spec(gmix.shape), _const_spec(win.shape),
                _const_spec(wa2.shape), _const_spec(ba.shape), _const_spec(freq.shape), _const_spec(sgn1.shape),
                _const_spec(sgn2.shape)] + late_specs
    out_shape = [jax.ShapeDtypeStruct((T, D_MODEL), F32),
                 jax.ShapeDtypeStruct((T, GLA_KW), BF16), jax.ShapeDtypeStruct((T, GLA_KW), BF16),
                 jax.ShapeDtypeStruct((T, GLA_WIDTH), BF16), jax.ShapeDtypeStruct((T, GLA_WIDTH), BF16),
                 jax.ShapeDtypeStruct((T, GLA_KW), F32),
                 ] + [jax.ShapeDtypeStruct((ATT_PAIRS, T, LANES), BF16)] * 3
    assert len(out_shape) == N_PROJ_OUTPUTS
    pair_major = pl.BlockSpec((ATT_PAIRS, tm, LANES), lambda i: (0, i, 0))
    out_specs = [row(s.shape[1]) for s in out_shape[:-3]] + [pair_major] * 3 + late_specs
    out_shape += [jax.ShapeDtypeStruct(w.shape, BF16) for w in late]
    res = pl.pallas_call(
        _ffn_proj_kernel, grid=(steps,), in_specs=in_specs, out_specs=out_specs, out_shape=out_shape,
        compiler_params=pltpu.CompilerParams(dimension_semantics=("arbitrary",),
                                             vmem_limit_bytes=VMEM_LIMIT_BYTES),
        name="ffn_proj")(x2d, pos, g1, w1, w3, w2, gmix, win, wa2, ba, freq, sgn1, sgn2, *late)
    return res[:N_PROJ_OUTPUTS], res[N_PROJ_OUTPUTS:]


def _gla_kernel(q_ref, k_ref, v_ref, la_ref, r_ref, gn_ref, ltri_ref, o_ref, s_ref):
    @pl.when(pl.program_id(1) == 0)
    def _():
        s_ref[...] = jnp.zeros_like(s_ref)

    C = GLA_CHUNK
    r_kk = lax.broadcasted_iota(jnp.int32, (GLA_KW, GLA_KW), 0) // C
    c_kk = lax.broadcasted_iota(jnp.int32, (GLA_KW, GLA_KW), 1) // C
    bd_k = r_kk == c_kk
    r_kv = lax.broadcasted_iota(jnp.int32, (GLA_KW, GLA_WIDTH), 0) // C
    c_kv = lax.broadcasted_iota(jnp.int32, (GLA_KW, GLA_WIDTH), 1) // GLA_DV
    bd_v = r_kv == c_kv
    causal = (lax.broadcasted_iota(jnp.int32, (C, GLA_KW), 1) % C
              <= lax.broadcasted_iota(jnp.int32, (C, GLA_KW), 0))
    ltri = ltri_ref[...]

    for hf in range(GLA_TILE // GLA_HALF):
        base = hf * GLA_HALF
        g = la_ref[base:base + GLA_HALF, :]
        g_hi = g.astype(BF16)
        g_lo = (g - g_hi.astype(F32)).astype(BF16)
        b = (jnp.dot(ltri, g_hi, preferred_element_type=F32)
             + jnp.dot(ltri, g_lo, preferred_element_type=F32))
        for c in range(GLA_HALF // C):
            lo = base + c * C
            bc = b[c * C:(c + 1) * C, :]
            bl = bc[C - 1:C, :]
            qc = q_ref[lo:lo + C, :].astype(F32)
            kc = k_ref[lo:lo + C, :].astype(F32)
            vc = v_ref[lo:lo + C, :]
            qd = (qc * jnp.exp(bc)).astype(BF16)
            k_inv = kc * jnp.exp(-bc)
            ki = k_inv.astype(BF16)
            kt = k_inv * jnp.exp(bl)
            kbd = jnp.where(bd_k, jnp.tile(ki, (GLA_HEADS, 1)), jnp.zeros((), BF16))
            a = lax.dot_general(qd, kbd, (((1,), (1,)), ((), ())), preferred_element_type=F32)
            a = jnp.where(causal, a, 0.0).astype(BF16)
            vbd = jnp.where(bd_v, jnp.tile(vc, (GLA_HEADS, 1)), jnp.zeros((), BF16))
            s_prev = s_ref[...]
            sbd = jnp.where(bd_v, jnp.tile(s_prev.astype(BF16), (1, GLA_HEADS)), jnp.zeros((), BF16))
            o = (jnp.dot(a, vbd, preferred_element_type=F32)
                 + jnp.dot(qd, sbd, preferred_element_type=F32))

            kt_t = kt.T.astype(BF16)
            dec = jnp.exp(bc.T[:, C - 1:C])
            u = jnp.concatenate(
                [jnp.dot(kt_t[h * GLA_DK:(h + 1) * GLA_DK, :], vc[:, h * GLA_DV:(h + 1) * GLA_DV],
                         preferred_element_type=F32) for h in range(GLA_HEADS)], axis=0)
            s_ref[...] = dec * s_prev + u

            gate = r_ref[lo:lo + C, :].astype(F32)
            for h in range(GLA_HEADS):
                hs = slice(h * GLA_DV, (h + 1) * GLA_DV)
                oh = _rms(o[:, hs], gn_ref[:, hs])
                o_ref[lo:lo + C, hs] = (oh * gate[:, hs]).astype(BF16)


def _gla(gq, gk, gv, la, gr, gn, ltri, batch):
    T = gq.shape[0]
    tiles = T // batch // GLA_TILE
    row = lambda n: pl.BlockSpec((GLA_TILE, n), lambda b, t: (b * tiles + t, 0))
    return pl.pallas_call(
        _gla_kernel, grid=(batch, tiles),
        in_specs=[row(GLA_KW), row(GLA_KW), row(GLA_WIDTH), row(GLA_KW), row(GLA_WIDTH),
                  _const_spec(gn.shape), _const_spec(ltri.shape)],
        out_specs=row(GLA_WIDTH),
        out_shape=jax.ShapeDtypeStruct((T, GLA_WIDTH), BF16),
        scratch_shapes=[pltpu.VMEM((GLA_KW, GLA_DV), F32)],
        compiler_params=pltpu.CompilerParams(dimension_semantics=("arbitrary", "arbitrary"),
                                             vmem_limit_bytes=VMEM_LIMIT_BYTES),
        name="gla")(gq, gk, gv, la, gr, gn, ltri)


def _attn_kernel(q_ref, k_ref, v_ref, o_ref, xf_ref, *scratch):
    B = ATT_BLOCK
    T = ATT_SUPER
    nd = len(ATT_DILATIONS)
    qd = dict(zip(ATT_DILATIONS[1:], scratch[:nd - 1]))
    kc = dict(zip(ATT_DILATIONS, scratch[nd - 1:2 * nd - 1]))
    vc = dict(zip(ATT_DILATIONS, scratch[2 * nd - 1:3 * nd - 1]))
    acc = scratch[3 * nd - 1:]
    acc_o = dict(zip(ATT_DILATIONS, acc[0::2]))
    acc_l = dict(zip(ATT_DILATIONS, acc[1::2]))
    tile = pl.program_id(2)

    def stage(src_ref, dst, halo):
        xf_ref[...] = src_ref[...].astype(F32)
        for d in ATT_DILATIONS[1:]:
            n = T // d
            for r in range(d):
                lo_row = r * (n + halo) + halo
                dst[d][lo_row:lo_row + n, :] = xf_ref[pl.ds(r, n, stride=d), :].astype(BF16)

    @pl.when(tile == 0)
    def _():
        zeros = jnp.zeros((B, LANES), BF16)
        for d in ATT_DILATIONS:
            for r in range(d):
                row = r * (T // d + B)
                kc[d][row:row + B, :] = zeros
                vc[d][row:row + B, :] = zeros

    stage(q_ref, qd, 0)
    stage(k_ref, kc, B)
    stage(v_ref, vc, B)
    kc[1][B:, :] = k_ref[...]
    vc[1][B:, :] = v_ref[...]

    qi = lax.broadcasted_iota(jnp.int32, (B, 2 * B), 0)
    ki = lax.broadcasted_iota(jnp.int32, (B, 2 * B), 1)
    band = (ki >= qi) & (ki <= qi + B)
    bias_band = jnp.where(band, 0.0, NEG).astype(F32)
    bias_first = jnp.where(band & ((ki >= B) | (tile > 0)), 0.0, NEG).astype(F32)
    lane = lax.broadcasted_iota(jnp.int32, (B, LANES), 1)
    lo = lane < ATT_HD
    zero_bf = jnp.zeros((), BF16)
    ones_v = jnp.ones((2 * B, LANES), BF16)

    def block(q_src, q_row, d, k_row, bias, rows_out):
        qp = q_src[q_row:q_row + B, :]
        kk = kc[d][k_row:k_row + 2 * B, :]
        vv = vc[d][k_row:k_row + 2 * B, :]
        qs = jnp.concatenate([jnp.where(lo, qp, zero_bf), jnp.where(lo, zero_bf, qp)], axis=0)
        s = lax.dot_general(qs, kk, (((1,), (1,)), ((), ())), preferred_element_type=F32)
        s = s + jnp.concatenate([bias, bias], axis=0)
        m = jnp.max(s, axis=-1, keepdims=True)
        p = jnp.exp2(s - m).astype(BF16)
        pv = jnp.dot(p, jnp.concatenate([vv, ones_v], axis=1), preferred_element_type=F32)
        den = jnp.where(lo, pv[:B, LANES:], pv[B:, LANES:])
        acc_o[d][rows_out, :] = jnp.where(lo, pv[:B, :LANES], pv[B:, :LANES]) / den
        acc_l[d][rows_out, :] = jnp.where(lo, m[:B], m[B:]) + jnp.log2(den)

    for d in ATT_DILATIONS:
        n = T // d
        for r in range(d):
            for jb in range(n // B):
                q_src, q_row = (q_ref, jb * B) if d == 1 else (qd[d], r * n + jb * B)
                rows_out = pl.ds(r + d * B * jb, B, stride=d) if d > 1 else pl.ds(jb * B, B)
                block(q_src, q_row, d, r * (n + B) + jb * B, bias_first if jb == 0 else bias_band,
                      rows_out)

    for c in range(T // ATT_COMBINE_ROWS):
        rows = slice(c * ATT_COMBINE_ROWS, (c + 1) * ATT_COMBINE_ROWS)
        ls = [acc_l[d][rows, :] for d in ATT_DILATIONS]
        mx = functools.reduce(jnp.maximum, ls)
        es = [jnp.exp2(l - mx) for l in ls]
        num = functools.reduce(jnp.add, [e * acc_o[d][rows, :] for e, d in zip(es, ATT_DILATIONS)])
        o_ref[rows, :] = (num / functools.reduce(jnp.add, es)).astype(o_ref.dtype)

    for d in ATT_DILATIONS:
        n = T // d
        for r in range(d):
            row = r * (n + B)
            kc[d][row:row + B, :] = kc[d][row + n:row + n + B, :]
            vc[d][row:row + B, :] = vc[d][row + n:row + n + B, :]


def _attn(q, k, v, batch, seq):
    T = q.shape[1]
    tiles = seq // ATT_SUPER
    spec = pl.BlockSpec((None, ATT_SUPER, LANES), lambda b, h, t: (h, b * tiles + t, 0))
    rows = lambda n, dt: pltpu.VMEM((n, LANES), dt)
    kv_rows = [ATT_SUPER + d * ATT_BLOCK for d in ATT_DILATIONS]
    scratch = ([rows(ATT_SUPER, F32)] + [rows(ATT_SUPER, BF16) for _ in ATT_DILATIONS[1:]]
               + [rows(n, BF16) for n in kv_rows] + [rows(n, BF16) for n in kv_rows]
               + [rows(ATT_SUPER, F32) for _ in range(2 * len(ATT_DILATIONS))])
    return pl.pallas_call(
        _attn_kernel, grid=(batch, ATT_PAIRS, tiles), in_specs=[spec, spec, spec],
        out_specs=spec, out_shape=jax.ShapeDtypeStruct((ATT_PAIRS, T, LANES), BF16), scratch_shapes=scratch,
        compiler_params=pltpu.CompilerParams(
            dimension_semantics=("arbitrary", "arbitrary", "arbitrary"),
            vmem_limit_bytes=VMEM_LIMIT_BYTES),
        name="attn")(q, k, v)


def _out_ffn_kernel(x1_ref, og_ref, oa_ref, ga_ref, wo_ref, g2_ref, w1_ref, w3_ref, w2_ref, gf_ref,
                    out_ref):
    oa = jnp.concatenate([oa_ref[c] for c in range(ATT_PAIRS)], axis=1)
    oa = _rms(oa.astype(F32), ga_ref[...]).astype(BF16)
    x2 = (x1_ref[...]
          + jnp.dot(og_ref[...], wo_ref[:GLA_WIDTH, :], preferred_element_type=F32)
          + jnp.dot(oa, wo_ref[GLA_WIDTH:, :], preferred_element_type=F32))
    h = _rms(x2, g2_ref[...]).astype(BF16)
    x3 = x2 + 0.5 * _swiglu_half_step(h, w1_ref, w3_ref, w2_ref)
    out_ref[...] = _rms(x3, gf_ref[...])


def _out_ffn(x1, og, oa, ga, wo, g2, w1, w3, w2, gf):
    T = x1.shape[0]
    tm = OUT_TOKEN_TILE
    row = lambda n: pl.BlockSpec((tm, n), lambda i: (i, 0))
    return pl.pallas_call(
        _out_ffn_kernel, grid=(T // tm,),
        in_specs=[row(D_MODEL), row(GLA_WIDTH),
                  pl.BlockSpec((ATT_PAIRS, tm, LANES), lambda i: (0, i, 0)), _const_spec(ga.shape),
                  _const_spec(wo.shape), _const_spec(g2.shape),
                  _const_spec(w1.shape), _const_spec(w3.shape), _const_spec(w2.shape),
                  _const_spec(gf.shape)],
        out_specs=row(D_MODEL),
        out_shape=jax.ShapeDtypeStruct((T, D_MODEL), F32),
        compiler_params=pltpu.CompilerParams(dimension_semantics=("arbitrary",),
                                             vmem_limit_bytes=VMEM_LIMIT_BYTES),
        name="out_ffn")(x1, og, oa, ga, wo, g2, w1, w3, w2, gf)


def _rope_tables():
    j = np.arange(LANES) % ATT_HD
    half = ROT_DIM // 2
    inv_freq = ROPE_THETA ** (-np.arange(0, ROT_DIM, 2, dtype=np.float32) / ROT_DIM)
    freq = np.where(j < ROT_DIM, inv_freq[j % half], 0.0).astype(np.float32)
    sgn1 = np.where(j < half, -1.0, 0.0).astype(np.float32)
    sgn2 = np.where((j >= half) & (j < ROT_DIM), 1.0, 0.0).astype(np.float32)
    return freq[None, :], sgn1[None, :], sgn2[None, :]


def _chunk_tril():
    i = np.arange(GLA_HALF)
    same = (i[:, None] // GLA_CHUNK) == (i[None, :] // GLA_CHUNK)
    return jnp.asarray((same & (i[None, :] <= i[:, None])).astype(np.float32), dtype=BF16)


def kernel(x, positions, ffn1_norm, ffn1_w1, ffn1_w3, ffn1_w2, mix_norm, w_in, gla_w_a2, gla_b_a,
           gla_out_norm, att_out_norm, w_out, ffn2_norm, ffn2_w1, ffn2_w3, ffn2_w2, final_norm):
    batch, seq, _ = x.shape
    T = batch * seq
    depth = ffn1_norm.shape[0]
    assert depth == 1, "the final norm is fused into the single layer's last kernel"
    freq, sgn1, sgn2 = (jnp.asarray(t) for t in _rope_tables())
    ltri = _chunk_tril()
    pos = positions.astype(F32).reshape(T // TOKEN_TILE, 1, TOKEN_TILE)
    xs = x.reshape(T, D_MODEL)

    for l in range(depth):
        wi = w_in[l]
        win = jnp.concatenate(
            [wi[:, :W_GLA], wi[:, W_GLA + GLA_RANK:], wi[:, W_GLA:W_GLA + GLA_RANK],
             jnp.zeros((D_MODEL, LANES - GLA_RANK), wi.dtype)], axis=1).astype(BF16)
        wa2 = jnp.pad(gla_w_a2[l].astype(BF16), ((0, LANES - GLA_RANK), (0, 0)))
        (x1, gq, gk, gv, gr, la, aq, ak, av), (w1b, w3b, w2b, wob) = _ffn_proj(
            xs, pos, ffn1_norm[l][None, :], ffn1_w1[l].astype(BF16), ffn1_w3[l].astype(BF16),
            ffn1_w2[l].astype(BF16), mix_norm[l][None, :], win, wa2, gla_b_a[l][None, :], freq, sgn1, sgn2,
            late=(ffn2_w1[l], ffn2_w3[l], ffn2_w2[l], w_out[l]))

        o_gla = _gla(gq, gk, gv, la, gr, gla_out_norm[l][None, :], ltri, batch)

        o_att = _attn(aq, ak, av, batch, seq)

        xs = _out_ffn(x1, o_gla, o_att, att_out_norm[l][None, :], wob, ffn2_norm[l][None, :], w1b, w3b,
                      w2b, final_norm[None, :])
    return xs.reshape(batch, seq, D_MODEL)
```

```python
import functools
import math

import jax
import jax.numpy as jnp
import numpy as np
from jax import lax
from jax.experimental import pallas as pl
from jax.experimental.pallas import tpu as pltpu

F32 = jnp.float32
BF16 = jnp.bfloat16

D_MODEL = 1024
D_FF = 2816
GLA_WIDTH = 512
GLA_HEADS = 4
GLA_DV = 128
GLA_DK = 64
GLA_KW = GLA_HEADS * GLA_DK
GLA_RANK = 16
GLA_TAU = 16.0
GLA_CHUNK = 64
ATT_WIDTH = 512
ATT_HEADS = 8
ATT_HD = 64
ROT_DIM = 16
ROPE_THETA = 500000.0
DILATED_PATTERNS = ((128, 1), (512, 4), (2048, 16))
ATT_BLOCK = 128
EPS = 1e-6

LANES = 128
SUBLANES = 8
BF16_SUBLANES = 16
VMEM_LIMIT_BYTES = 56 * 1024 * 1024

TOKEN_TILE = 512
OUT_TOKEN_TILE = 1024
ROW_GROUP = 256
MXU_TILE = 256
FF_SPLITS = (0, (D_FF // MXU_TILE // 2) * MXU_TILE, D_FF)
assert D_FF % MXU_TILE == 0
GLA_TILE = 2048
GLA_HALF = 256
ATT_DILATIONS = tuple(sorted(d for _, d in DILATED_PATTERNS))
ATT_SUPER = max(ATT_DILATIONS) * ATT_BLOCK
ATT_COMBINE_ROWS = 256
NEG = -1e30
LOG2E = math.log2(math.e)

assert all(w // d == ATT_BLOCK for w, d in DILATED_PATTERNS)
assert ATT_DILATIONS[0] == 1 and all(ATT_SUPER % (d * ATT_BLOCK) == 0 for d in ATT_DILATIONS)


def _const_spec(shape):
    nd = len(shape)
    return pl.BlockSpec(shape, lambda *_: (0,) * nd, pipeline_mode=pl.Buffered(1))


def _rms(x, g):
    return x * lax.rsqrt(jnp.mean(x * x, axis=-1, keepdims=True) + EPS) * g


def _swiglu_half_step(h, w1_ref, w3_ref, w2_ref):
    acc = None
    for lo, hi in zip(FF_SPLITS[:-1], FF_SPLITS[1:]):
        sl = slice(lo, hi)
        a = jnp.dot(h, w1_ref[:, sl], preferred_element_type=F32)
        b = jnp.dot(h, w3_ref[:, sl], preferred_element_type=F32)
        g = (a / (1.0 + jnp.exp(-a)) * b).astype(BF16)
        part = jnp.dot(g, w2_ref[sl, :], preferred_element_type=F32)
        acc = part if acc is None else acc + part
    return acc


N_PROJ_OUTPUTS = 9


def _ffn_proj_kernel(x_ref, pos_ref, g1_ref, w1_ref, w3_ref, w2_ref, gmix_ref, wgla_ref, wga_ref,
                     wa2_ref, ba_ref, watt_ref, freq_ref, sgn1_ref, sgn2_ref, *rest):
    n_late = (len(rest) - N_PROJ_OUTPUTS) // 2
    late_in = rest[:n_late]
    x1_ref, gq_ref, gk_ref, gv_ref, gr_ref, la_ref, aq_ref, ak_ref, av_ref = (
        rest[n_late:n_late + N_PROJ_OUTPUTS])
    late_out = rest[n_late + N_PROJ_OUTPUTS:]
    for src, dst in zip(late_in, late_out):
        dst[...] = src[...].astype(BF16)

    groups = [slice(r, r + ROW_GROUP) for r in range(0, TOKEN_TILE, ROW_GROUP)]
    x1s = []
    for rows in groups:
        x = x_ref[rows, :]
        h = _rms(x, g1_ref[...]).astype(BF16)
        x1 = x + 0.5 * _swiglu_half_step(h, w1_ref, w3_ref, w2_ref)
        x1_ref[rows, :] = x1
        x1s.append(x1)

    pos_col = jnp.broadcast_to(pos_ref[...], (SUBLANES, TOKEN_TILE)).T[:, 0:1]
    for rows, x1 in zip(groups, x1s):
        h2 = _rms(x1, gmix_ref[...]).astype(BF16)
        pa = jnp.dot(h2, watt_ref[...], preferred_element_type=F32)
        av_ref[rows, :] = pa[:, 2 * ATT_WIDTH:].astype(BF16)

        ang = pos_col[rows, :] * freq_ref[...]
        cos = jnp.cos(ang)
        sin = jnp.sin(ang)
        s1 = sin * sgn1_ref[...]
        s2 = sin * sgn2_ref[...]
        half = ROT_DIM // 2
        for off, ref, scale in ((0, aq_ref, ATT_HD ** -0.5 * LOG2E), (ATT_WIDTH, ak_ref, 1.0)):
            for cg in range(ATT_WIDTH // LANES):
                t = pa[:, off + cg * LANES: off + (cg + 1) * LANES]
                r = t * cos + pltpu.roll(t, LANES - half, 1) * s1 + pltpu.roll(t, half, 1) * s2
                ref[rows, cg * LANES:(cg + 1) * LANES] = (r * scale).astype(BF16)

        ga = jnp.dot(h2, wga_ref[...], preferred_element_type=F32).astype(BF16)
        z = jnp.dot(ga, wa2_ref[...], preferred_element_type=F32) + ba_ref[...]
        la_ref[rows, :] = (jnp.minimum(z, 0.0) - jnp.log(1.0 + jnp.exp(-jnp.abs(z)))) * (1.0 / GLA_TAU)

        pg = jnp.dot(h2, wgla_ref[...], preferred_element_type=F32)
        gq_ref[rows, :] = (pg[:, :GLA_KW] * (GLA_DK ** -0.5)).astype(BF16)
        gk_ref[rows, :] = pg[:, GLA_KW:2 * GLA_KW].astype(BF16)
        gv_ref[rows, :] = pg[:, 2 * GLA_KW:2 * GLA_KW + GLA_WIDTH].astype(BF16)
        gr = pg[:, 2 * GLA_KW + GLA_WIDTH:]
        gr_ref[rows, :] = (gr / (1.0 + jnp.exp(-gr))).astype(BF16)


def _late_chunk_spec(shape, steps):
    rows, cols = shape
    share = 1 if (rows // steps) % BF16_SUBLANES == 0 else 2
    assert rows % (steps // share) == 0 and (rows * share // steps) % BF16_SUBLANES == 0
    return pl.BlockSpec((rows * share // steps, cols), lambda i: (i // share, 0))


def _ffn_proj(x2d, pos, g1, w1, w3, w2, gmix, wgla, wga, wa2, ba, watt, freq, sgn1, sgn2, late):
    T = x2d.shape[0]
    tm = TOKEN_TILE
    steps = T // tm
    row = lambda n: pl.BlockSpec((tm, n), lambda i: (i, 0))
    pos_spec = pl.BlockSpec((None, 1, tm), lambda i: (i, 0, 0))
    late_specs = [_late_chunk_spec(w.shape, steps) for w in late]
    in_specs = [row(D_MODEL), pos_spec, _const_spec(g1.shape), _const_spec(w1.shape), _const_spec(w3.shape),
                _const_spec(w2.shape), _const_spec(gmix.shape), _const_spec(wgla.shape),
                _const_spec(wga.shape), _const_spec(wa2.shape), _const_spec(ba.shape),
                _const_spec(watt.shape), _const_spec(freq.shape), _const_spec(sgn1.shape),
                _const_spec(sgn2.shape)] + late_specs
    out_shape = [jax.ShapeDtypeStruct((T, D_MODEL), F32),
                 jax.ShapeDtypeStruct((T, GLA_KW), BF16), jax.ShapeDtypeStruct((T, GLA_KW), BF16),
                 jax.ShapeDtypeStruct((T, GLA_WIDTH), BF16), jax.ShapeDtypeStruct((T, GLA_WIDTH), BF16),
                 jax.ShapeDtypeStruct((T, GLA_KW), F32),
                 jax.ShapeDtypeStruct((T, ATT_WIDTH), BF16), jax.ShapeDtypeStruct((T, ATT_WIDTH), BF16),
                 jax.ShapeDtypeStruct((T, ATT_WIDTH), BF16)]
    assert len(out_shape) == N_PROJ_OUTPUTS
    out_specs = [row(s.shape[1]) for s in out_shape] + late_specs
    out_shape += [jax.ShapeDtypeStruct(w.shape, BF16) for w in late]
    res = pl.pallas_call(
        _ffn_proj_kernel, grid=(steps,), in_specs=in_specs, out_specs=out_specs, out_shape=out_shape,
        compiler_params=pltpu.CompilerParams(dimension_semantics=("arbitrary",),
                                             vmem_limit_bytes=VMEM_LIMIT_BYTES),
        name="ffn_proj")(x2d, pos, g1, w1, w3, w2, gmix, wgla, wga, wa2, ba, watt, freq, sgn1, sgn2,
                         *late)
    return res[:N_PROJ_OUTPUTS], res[N_PROJ_OUTPUTS:]


def _gla_kernel(q_ref, k_ref, v_ref, la_ref, r_ref, gn_ref, ltri_ref, o_ref, s_ref):
    @pl.when(pl.program_id(1) == 0)
    def _():
        s_ref[...] = jnp.zeros_like(s_ref)

    C = GLA_CHUNK
    r_kk = lax.broadcasted_iota(jnp.int32, (GLA_KW, GLA_KW), 0) // C
    c_kk = lax.broadcasted_iota(jnp.int32, (GLA_KW, GLA_KW), 1) // C
    bd_k = r_kk == c_kk
    r_kv = lax.broadcasted_iota(jnp.int32, (GLA_KW, GLA_WIDTH), 0) // C
    c_kv = lax.broadcasted_iota(jnp.int32, (GLA_KW, GLA_WIDTH), 1) // GLA_DV
    bd_v = r_kv == c_kv
    causal = (lax.broadcasted_iota(jnp.int32, (C, GLA_KW), 1) % C
              <= lax.broadcasted_iota(jnp.int32, (C, GLA_KW), 0))
    ltri = ltri_ref[...]

    for hf in range(GLA_TILE // GLA_HALF):
        base = hf * GLA_HALF
        g = la_ref[base:base + GLA_HALF, :]
        g_hi = g.astype(BF16)
        g_lo = (g - g_hi.astype(F32)).astype(BF16)
        b = (jnp.dot(ltri, g_hi, preferred_element_type=F32)
             + jnp.dot(ltri, g_lo, preferred_element_type=F32))
        for c in range(GLA_HALF // C):
            lo = base + c * C
            bc = b[c * C:(c + 1) * C, :]
            bl = bc[C - 1:C, :]
            qc = q_ref[lo:lo + C, :].astype(F32)
            kc = k_ref[lo:lo + C, :].astype(F32)
            vc = v_ref[lo:lo + C, :]
            qd = (qc * jnp.exp(bc)).astype(BF16)
            k_inv = kc * jnp.exp(-bc)
            ki = k_inv.astype(BF16)
            kt = k_inv * jnp.exp(bl)
            kbd = jnp.where(bd_k, jnp.tile(ki, (GLA_HEADS, 1)), jnp.zeros((), BF16))
            a = lax.dot_general(qd, kbd, (((1,), (1,)), ((), ())), preferred_element_type=F32)
            a = jnp.where(causal, a, 0.0).astype(BF16)
            vbd = jnp.where(bd_v, jnp.tile(vc, (GLA_HEADS, 1)), jnp.zeros((), BF16))
            s_prev = s_ref[...]
            sbd = jnp.where(bd_v, jnp.tile(s_prev.astype(BF16), (1, GLA_HEADS)), jnp.zeros((), BF16))
            o = (jnp.dot(a, vbd, preferred_element_type=F32)
                 + jnp.dot(qd, sbd, preferred_element_type=F32))

            kt_t = kt.T.astype(BF16)
            dec = jnp.exp(bc.T[:, C - 1:C])
            u = jnp.concatenate(
                [jnp.dot(kt_t[h * GLA_DK:(h + 1) * GLA_DK, :], vc[:, h * GLA_DV:(h + 1) * GLA_DV],
                         preferred_element_type=F32) for h in range(GLA_HEADS)], axis=0)
            s_ref[...] = dec * s_prev + u

            gate = r_ref[lo:lo + C, :].astype(F32)
            for h in range(GLA_HEADS):
                hs = slice(h * GLA_DV, (h + 1) * GLA_DV)
                oh = _rms(o[:, hs], gn_ref[:, hs])
                o_ref[lo:lo + C, hs] = (oh * gate[:, hs]).astype(BF16)


def _gla(gq, gk, gv, la, gr, gn, ltri, batch):
    T = gq.shape[0]
    tiles = T // batch // GLA_TILE
    row = lambda n: pl.BlockSpec((GLA_TILE, n), lambda b, t: (b * tiles + t, 0))
    return pl.pallas_call(
        _gla_kernel, grid=(batch, tiles),
        in_specs=[row(GLA_KW), row(GLA_KW), row(GLA_WIDTH), row(GLA_KW), row(GLA_WIDTH),
                  _const_spec(gn.shape), _const_spec(ltri.shape)],
        out_specs=row(GLA_WIDTH),
        out_shape=jax.ShapeDtypeStruct((T, GLA_WIDTH), BF16),
        scratch_shapes=[pltpu.VMEM((GLA_KW, GLA_DV), F32)],
        compiler_params=pltpu.CompilerParams(dimension_semantics=("arbitrary", "arbitrary"),
                                             vmem_limit_bytes=VMEM_LIMIT_BYTES),
        name="gla")(gq, gk, gv, la, gr, gn, ltri)


def _attn_kernel(q_ref, k_ref, v_ref, o_ref, xf_ref, *scratch):
    B = ATT_BLOCK
    T = ATT_SUPER
    nd = len(ATT_DILATIONS)
    qd = dict(zip(ATT_DILATIONS[1:], scratch[:nd - 1]))
    kc = dict(zip(ATT_DILATIONS, scratch[nd - 1:2 * nd - 1]))
    vc = dict(zip(ATT_DILATIONS, scratch[2 * nd - 1:3 * nd - 1]))
    acc = scratch[3 * nd - 1:]
    acc_o = dict(zip(ATT_DILATIONS, acc[0::2]))
    acc_l = dict(zip(ATT_DILATIONS, acc[1::2]))
    tile = pl.program_id(2)

    def stage(src_ref, dst, halo):
        xf_ref[...] = src_ref[...].astype(F32)
        for d in ATT_DILATIONS[1:]:
            n = T // d
            for r in range(d):
                lo_row = r * (n + halo) + halo
                dst[d][lo_row:lo_row + n, :] = xf_ref[pl.ds(r, n, stride=d), :].astype(BF16)

    @pl.when(tile == 0)
    def _():
        zeros = jnp.zeros((B, LANES), BF16)
        for d in ATT_DILATIONS:
            for r in range(d):
                row = r * (T // d + B)
                kc[d][row:row + B, :] = zeros
                vc[d][row:row + B, :] = zeros

    stage(q_ref, qd, 0)
    stage(k_ref, kc, B)
    stage(v_ref, vc, B)
    kc[1][B:, :] = k_ref[...]
    vc[1][B:, :] = v_ref[...]

    qi = lax.broadcasted_iota(jnp.int32, (B, 2 * B), 0)
    ki = lax.broadcasted_iota(jnp.int32, (B, 2 * B), 1)
    band = (ki >= qi) & (ki <= qi + B)
    bias_band = jnp.where(band, 0.0, NEG).astype(F32)
    bias_first = jnp.where(band & ((ki >= B) | (tile > 0)), 0.0, NEG).astype(F32)
    lane = lax.broadcasted_iota(jnp.int32, (B, LANES), 1)
    lo = lane < ATT_HD
    zero_bf = jnp.zeros((), BF16)
    ones_v = jnp.ones((2 * B, LANES), BF16)

    def block(q_src, q_row, d, k_row, bias, rows_out):
        qp = q_src[q_row:q_row + B, :]
        kk = kc[d][k_row:k_row + 2 * B, :]
        vv = vc[d][k_row:k_row + 2 * B, :]
        qs = jnp.concatenate([jnp.where(lo, qp, zero_bf), jnp.where(lo, zero_bf, qp)], axis=0)
        s = lax.dot_general(qs, kk, (((1,), (1,)), ((), ())), preferred_element_type=F32)
        s = s + jnp.concatenate([bias, bias], axis=0)
        m = jnp.max(s, axis=-1, keepdims=True)
        p = jnp.exp2(s - m).astype(BF16)
        pv = jnp.dot(p, jnp.concatenate([vv, ones_v], axis=1), preferred_element_type=F32)
        den = jnp.where(lo, pv[:B, LANES:], pv[B:, LANES:])
        acc_o[d][rows_out, :] = jnp.where(lo, pv[:B, :LANES], pv[B:, :LANES]) / den
        acc_l[d][rows_out, :] = jnp.where(lo, m[:B], m[B:]) + jnp.log2(den)

    for d in ATT_DILATIONS:
        n = T // d
        for r in range(d):
            for jb in range(n // B):
                q_src, q_row = (q_ref, jb * B) if d == 1 else (qd[d], r * n + jb * B)
                rows_out = pl.ds(r + d * B * jb, B, stride=d) if d > 1 else pl.ds(jb * B, B)
                block(q_src, q_row, d, r * (n + B) + jb * B, bias_first if jb == 0 else bias_band,
                      rows_out)

    for c in range(T // ATT_COMBINE_ROWS):
        rows = slice(c * ATT_COMBINE_ROWS, (c + 1) * ATT_COMBINE_ROWS)
        ls = [acc_l[d][rows, :] for d in ATT_DILATIONS]
        mx = functools.reduce(jnp.maximum, ls)
        es = [jnp.exp2(l - mx) for l in ls]
        num = functools.reduce(jnp.add, [e * acc_o[d][rows, :] for e, d in zip(es, ATT_DILATIONS)])
        o_ref[rows, :] = (num / functools.reduce(jnp.add, es)).astype(o_ref.dtype)

    for d in ATT_DILATIONS:
        n = T // d
        for r in range(d):
            row = r * (n + B)
            kc[d][row:row + B, :] = kc[d][row + n:row + n + B, :]
            vc[d][row:row + B, :] = vc[d][row + n:row + n + B, :]


def _attn(q, k, v, batch, seq):
    T = q.shape[0]
    tiles = seq // ATT_SUPER
    spec = pl.BlockSpec((ATT_SUPER, LANES), lambda b, h, t: (b * tiles + t, h))
    rows = lambda n, dt: pltpu.VMEM((n, LANES), dt)
    kv_rows = [ATT_SUPER + d * ATT_BLOCK for d in ATT_DILATIONS]
    scratch = ([rows(ATT_SUPER, F32)] + [rows(ATT_SUPER, BF16) for _ in ATT_DILATIONS[1:]]
               + [rows(n, BF16) for n in kv_rows] + [rows(n, BF16) for n in kv_rows]
               + [rows(ATT_SUPER, F32) for _ in range(2 * len(ATT_DILATIONS))])
    return pl.pallas_call(
        _attn_kernel, grid=(batch, ATT_WIDTH // LANES, tiles), in_specs=[spec, spec, spec],
        out_specs=spec, out_shape=jax.ShapeDtypeStruct((T, ATT_WIDTH), BF16), scratch_shapes=scratch,
        compiler_params=pltpu.CompilerParams(
            dimension_semantics=("arbitrary", "arbitrary", "arbitrary"),
            vmem_limit_bytes=VMEM_LIMIT_BYTES),
        name="attn")(q, k, v)


def _out_ffn_kernel(x1_ref, og_ref, oa_ref, ga_ref, wo_ref, g2_ref, w1_ref, w3_ref, w2_ref, gf_ref,
                    out_ref):
    groups = [slice(r, r + ROW_GROUP) for r in range(0, OUT_TOKEN_TILE, ROW_GROUP)]
    x2s = []
    for rows in groups:
        oa = _rms(oa_ref[rows, :].astype(F32), ga_ref[...]).astype(BF16)
        x2s.append(x1_ref[rows, :]
                   + jnp.dot(og_ref[rows, :], wo_ref[:GLA_WIDTH, :], preferred_element_type=F32)
                   + jnp.dot(oa, wo_ref[GLA_WIDTH:, :], preferred_element_type=F32))
    for rows, x2 in zip(groups, x2s):
        h = _rms(x2, g2_ref[...]).astype(BF16)
        x3 = x2 + 0.5 * _swiglu_half_step(h, w1_ref, w3_ref, w2_ref)
        out_ref[rows, :] = _rms(x3, gf_ref[...])


def _out_ffn(x1, og, oa, ga, wo, g2, w1, w3, w2, gf):
    T = x1.shape[0]
    tm = OUT_TOKEN_TILE
    row = lambda n: pl.BlockSpec((tm, n), lambda i: (i, 0))
    return pl.pallas_call(
        _out_ffn_kernel, grid=(T // tm,),
        in_specs=[row(D_MODEL), row(GLA_WIDTH), row(ATT_WIDTH), _const_spec(ga.shape),
                  _const_spec(wo.shape), _const_spec(g2.shape),
                  _const_spec(w1.shape), _const_spec(w3.shape), _const_spec(w2.shape),
                  _const_spec(gf.shape)],
        out_specs=row(D_MODEL),
        out_shape=jax.ShapeDtypeStruct((T, D_MODEL), F32),
        compiler_params=pltpu.CompilerParams(dimension_semantics=("arbitrary",),
                                             vmem_limit_bytes=VMEM_LIMIT_BYTES),
        name="out_ffn")(x1, og, oa, ga, wo, g2, w1, w3, w2, gf)


def _rope_tables():
    j = np.arange(LANES) % ATT_HD
    half = ROT_DIM // 2
    inv_freq = ROPE_THETA ** (-np.arange(0, ROT_DIM, 2, dtype=np.float32) / ROT_DIM)
    freq = np.where(j < ROT_DIM, inv_freq[j % half], 0.0).astype(np.float32)
    sgn1 = np.where(j < half, -1.0, 0.0).astype(np.float32)
    sgn2 = np.where((j >= half) & (j < ROT_DIM), 1.0, 0.0).astype(np.float32)
    return freq[None, :], sgn1[None, :], sgn2[None, :]


def _chunk_tril():
    i = np.arange(GLA_HALF)
    same = (i[:, None] // GLA_CHUNK) == (i[None, :] // GLA_CHUNK)
    return jnp.asarray((same & (i[None, :] <= i[:, None])).astype(np.float32), dtype=BF16)


def kernel(x, positions, ffn1_norm, ffn1_w1, ffn1_w3, ffn1_w2, mix_norm, w_in, gla_w_a2, gla_b_a,
           gla_out_norm, att_out_norm, w_out, ffn2_norm, ffn2_w1, ffn2_w3, ffn2_w2, final_norm):
    batch, seq, _ = x.shape
    T = batch * seq
    depth = ffn1_norm.shape[0]
    assert depth == 1, "the final norm is fused into the single layer's last kernel"
    freq, sgn1, sgn2 = (jnp.asarray(t) for t in _rope_tables())
    ltri = _chunk_tril()
    pos = positions.astype(F32).reshape(T // TOKEN_TILE, 1, TOKEN_TILE)
    xs = x.reshape(T, D_MODEL)

    o_gla_end = 2 * GLA_KW + 2 * GLA_WIDTH
    o_ga_end = o_gla_end + GLA_RANK
    for l in range(depth):
        wi = w_in[l]
        wgla = wi[:, :o_gla_end].astype(BF16)
        wga = jnp.pad(wi[:, o_gla_end:o_ga_end].astype(BF16), ((0, 0), (0, LANES - GLA_RANK)))
        watt = wi[:, o_ga_end:].astype(BF16)
        wa2 = jnp.pad(gla_w_a2[l].astype(BF16), ((0, LANES - GLA_RANK), (0, 0)))
        (x1, gq, gk, gv, gr, la, aq, ak, av), (w1b, w3b, w2b, wob) = _ffn_proj(
            xs, pos, ffn1_norm[l][None, :], ffn1_w1[l].astype(BF16), ffn1_w3[l].astype(BF16),
            ffn1_w2[l].astype(BF16), mix_norm[l][None, :], wgla, wga, wa2, gla_b_a[l][None, :], watt,
            freq, sgn1, sgn2, late=(ffn2_w1[l], ffn2_w3[l], ffn2_w2[l], w_out[l]))

        o_gla = _gla(gq, gk, gv, la, gr, gla_out_norm[l][None, :], ltri, batch)

        o_att = _attn(aq, ak, av, batch, seq)

        xs = _out_ffn(x1, o_gla, o_att, att_out_norm[l][None, :], wob, ffn2_norm[l][None, :], w1b, w3b,
                      w2b, final_norm[None, :])
    return xs.reshape(batch, seq, D_MODEL)
```

```python
import functools
import math

import jax
import jax.numpy as jnp
import numpy as np
from jax import lax
from jax.experimental import pallas as pl
from jax.experimental.pallas import tpu as pltpu

F32 = jnp.float32
BF16 = jnp.bfloat16

D_MODEL = 1024
D_FF = 2816
GLA_WIDTH = 512
GLA_HEADS = 4
GLA_DV = 128
GLA_DK = 64
GLA_KW = GLA_HEADS * GLA_DK
GLA_RANK = 16
GLA_TAU = 16.0
GLA_CHUNK = 64
ATT_WIDTH = 512
ATT_HEADS = 8
ATT_HD = 64
ROT_DIM = 16
ROPE_THETA = 500000.0
DILATED_PATTERNS = ((128, 1), (512, 4), (2048, 16))
ATT_BLOCK = 128
EPS = 1e-6

LANES = 128
SUBLANES = 8
BF16_SUBLANES = 16
VMEM_LIMIT_BYTES = 56 * 1024 * 1024
FFN_PROJ_VMEM_LIMIT_BYTES = 60 * 1024 * 1024

TOKEN_TILE = 1024
OUT_TOKEN_TILE = 1024
ROW_GROUP = 256
MXU_TILE = 256
FF_SPLITS = (0, (D_FF // MXU_TILE // 2) * MXU_TILE, D_FF)
assert D_FF % MXU_TILE == 0
GLA_TILE = 2048
GLA_HALF = 256
ATT_DILATIONS = tuple(sorted(d for _, d in DILATED_PATTERNS))
ATT_SUPER = max(ATT_DILATIONS) * ATT_BLOCK
ATT_COMBINE_ROWS = 256
NEG = -1e30
LOG2E = math.log2(math.e)

assert all(w // d == ATT_BLOCK for w, d in DILATED_PATTERNS)
assert ATT_DILATIONS[0] == 1 and all(ATT_SUPER % (d * ATT_BLOCK) == 0 for d in ATT_DILATIONS)


def _const_spec(shape):
    nd = len(shape)
    return pl.BlockSpec(shape, lambda *_: (0,) * nd, pipeline_mode=pl.Buffered(1))


def _rms(x, g):
    return x * lax.rsqrt(jnp.mean(x * x, axis=-1, keepdims=True) + EPS) * g


def _swiglu_half_step(h, w1_ref, w3_ref, w2_ref):
    acc = None
    for lo, hi in zip(FF_SPLITS[:-1], FF_SPLITS[1:]):
        sl = slice(lo, hi)
        a = jnp.dot(h, w1_ref[:, sl], preferred_element_type=F32)
        b = jnp.dot(h, w3_ref[:, sl], preferred_element_type=F32)
        g = (a / (1.0 + jnp.exp(-a)) * b).astype(BF16)
        part = jnp.dot(g, w2_ref[sl, :], preferred_element_type=F32)
        acc = part if acc is None else acc + part
    return acc


def _ffn_proj_kernel(x_ref, pos_ref, g1_ref, w1_ref, w3_ref, w2_ref, gmix_ref, wgla_ref, wga_ref,
                     wa2_ref, ba_ref, watt_ref, freq_ref, sgn1_ref, sgn2_ref,
                     x1_ref, gq_ref, gk_ref, gv_ref, gr_ref, la_ref, aq_ref, ak_ref, av_ref):
    groups = [slice(r, r + ROW_GROUP) for r in range(0, TOKEN_TILE, ROW_GROUP)]
    x1s = []
    for rows in groups:
        x = x_ref[rows, :]
        h = _rms(x, g1_ref[...]).astype(BF16)
        x1 = x + 0.5 * _swiglu_half_step(h, w1_ref, w3_ref, w2_ref)
        x1_ref[rows, :] = x1
        x1s.append(x1)

    pos_col = jnp.broadcast_to(pos_ref[...], (SUBLANES, TOKEN_TILE)).T[:, 0:1]
    for rows, x1 in zip(groups, x1s):
        h2 = _rms(x1, gmix_ref[...]).astype(BF16)
        pa = jnp.dot(h2, watt_ref[...], preferred_element_type=F32)
        av_ref[rows, :] = pa[:, 2 * ATT_WIDTH:].astype(BF16)

        ang = pos_col[rows, :] * freq_ref[...]
        cos = jnp.cos(ang)
        sin = jnp.sin(ang)
        s1 = sin * sgn1_ref[...]
        s2 = sin * sgn2_ref[...]
        half = ROT_DIM // 2
        for off, ref, scale in ((0, aq_ref, ATT_HD ** -0.5 * LOG2E), (ATT_WIDTH, ak_ref, 1.0)):
            for cg in range(ATT_WIDTH // LANES):
                t = pa[:, off + cg * LANES: off + (cg + 1) * LANES]
                r = t * cos + pltpu.roll(t, LANES - half, 1) * s1 + pltpu.roll(t, half, 1) * s2
                ref[rows, cg * LANES:(cg + 1) * LANES] = (r * scale).astype(BF16)

        ga = jnp.dot(h2, wga_ref[...], preferred_element_type=F32).astype(BF16)
        z = jnp.dot(ga, wa2_ref[...], preferred_element_type=F32) + ba_ref[...]
        la_ref[rows, :] = (jnp.minimum(z, 0.0) - jnp.log(1.0 + jnp.exp(-jnp.abs(z)))) * (1.0 / GLA_TAU)

        pg = jnp.dot(h2, wgla_ref[...], preferred_element_type=F32)
        gq_ref[rows, :] = (pg[:, :GLA_KW] * (GLA_DK ** -0.5)).astype(BF16)
        gk_ref[rows, :] = pg[:, GLA_KW:2 * GLA_KW].astype(BF16)
        gv_ref[rows, :] = pg[:, 2 * GLA_KW:2 * GLA_KW + GLA_WIDTH].astype(BF16)
        gr = pg[:, 2 * GLA_KW + GLA_WIDTH:]
        gr_ref[rows, :] = (gr / (1.0 + jnp.exp(-gr))).astype(BF16)


def _ffn_proj(x2d, pos, g1, w1, w3, w2, gmix, wgla, wga, wa2, ba, watt, freq, sgn1, sgn2):
    T = x2d.shape[0]
    tm = TOKEN_TILE
    row = lambda n: pl.BlockSpec((tm, n), lambda i: (i, 0))
    pos_spec = pl.BlockSpec((None, 1, tm), lambda i: (i, 0, 0))
    in_specs = [row(D_MODEL), pos_spec, _const_spec(g1.shape), _const_spec(w1.shape), _const_spec(w3.shape),
                _const_spec(w2.shape), _const_spec(gmix.shape), _const_spec(wgla.shape),
                _const_spec(wga.shape), _const_spec(wa2.shape), _const_spec(ba.shape),
                _const_spec(watt.shape), _const_spec(freq.shape), _const_spec(sgn1.shape),
                _const_spec(sgn2.shape)]
    out_shape = [jax.ShapeDtypeStruct((T, D_MODEL), F32),
                 jax.ShapeDtypeStruct((T, GLA_KW), BF16), jax.ShapeDtypeStruct((T, GLA_KW), BF16),
                 jax.ShapeDtypeStruct((T, GLA_WIDTH), BF16), jax.ShapeDtypeStruct((T, GLA_WIDTH), BF16),
                 jax.ShapeDtypeStruct((T, GLA_KW), F32),
                 jax.ShapeDtypeStruct((T, ATT_WIDTH), BF16), jax.ShapeDtypeStruct((T, ATT_WIDTH), BF16),
                 jax.ShapeDtypeStruct((T, ATT_WIDTH), BF16)]
    return pl.pallas_call(
        _ffn_proj_kernel, grid=(T // tm,), in_specs=in_specs,
        out_specs=[row(s.shape[1]) for s in out_shape], out_shape=out_shape,
        compiler_params=pltpu.CompilerParams(dimension_semantics=("arbitrary",),
                                             vmem_limit_bytes=FFN_PROJ_VMEM_LIMIT_BYTES),
        name="ffn_proj")(x2d, pos, g1, w1, w3, w2, gmix, wgla, wga, wa2, ba, watt, freq, sgn1, sgn2)


def _gla_kernel(q_ref, k_ref, v_ref, la_ref, r_ref, gn_ref, ltri_ref, *rest):
    n_late = (len(rest) - 2) // 2
    late_in, o_ref, late_out, s_ref = rest[:n_late], rest[n_late], rest[n_late + 1:-1], rest[-1]
    for src, dst in zip(late_in, late_out):
        dst[...] = src[...].astype(BF16)

    @pl.when(pl.program_id(1) == 0)
    def _():
        s_ref[...] = jnp.zeros_like(s_ref)

    C = GLA_CHUNK
    r_kk = lax.broadcasted_iota(jnp.int32, (GLA_KW, GLA_KW), 0) // C
    c_kk = lax.broadcasted_iota(jnp.int32, (GLA_KW, GLA_KW), 1) // C
    bd_k = r_kk == c_kk
    r_kv = lax.broadcasted_iota(jnp.int32, (GLA_KW, GLA_WIDTH), 0) // C
    c_kv = lax.broadcasted_iota(jnp.int32, (GLA_KW, GLA_WIDTH), 1) // GLA_DV
    bd_v = r_kv == c_kv
    causal = (lax.broadcasted_iota(jnp.int32, (C, GLA_KW), 1) % C
              <= lax.broadcasted_iota(jnp.int32, (C, GLA_KW), 0))
    ltri = ltri_ref[...]

    for hf in range(GLA_TILE // GLA_HALF):
        base = hf * GLA_HALF
        g = la_ref[base:base + GLA_HALF, :]
        g_hi = g.astype(BF16)
        g_lo = (g - g_hi.astype(F32)).astype(BF16)
        b = (jnp.dot(ltri, g_hi, preferred_element_type=F32)
             + jnp.dot(ltri, g_lo, preferred_element_type=F32))
        for c in range(GLA_HALF // C):
            lo = base + c * C
            bc = b[c * C:(c + 1) * C, :]
            bl = bc[C - 1:C, :]
            qc = q_ref[lo:lo + C, :].astype(F32)
            kc = k_ref[lo:lo + C, :].astype(F32)
            vc = v_ref[lo:lo + C, :]
            qd = (qc * jnp.exp(bc)).astype(BF16)
            k_inv = kc * jnp.exp(-bc)
            ki = k_inv.astype(BF16)
            kt = k_inv * jnp.exp(bl)
            kbd = jnp.where(bd_k, jnp.tile(ki, (GLA_HEADS, 1)), jnp.zeros((), BF16))
            a = lax.dot_general(qd, kbd, (((1,), (1,)), ((), ())), preferred_element_type=F32)
            a = jnp.where(causal, a, 0.0).astype(BF16)
            vbd = jnp.where(bd_v, jnp.tile(vc, (GLA_HEADS, 1)), jnp.zeros((), BF16))
            s_prev = s_ref[...]
            sbd = jnp.where(bd_v, jnp.tile(s_prev.astype(BF16), (1, GLA_HEADS)), jnp.zeros((), BF16))
            o = (jnp.dot(a, vbd, preferred_element_type=F32)
                 + jnp.dot(qd, sbd, preferred_element_type=F32))

            kt_t = kt.T.astype(BF16)
            dec = jnp.exp(bc.T[:, C - 1:C])
            u = jnp.concatenate(
                [jnp.dot(kt_t[h * GLA_DK:(h + 1) * GLA_DK, :], vc[:, h * GLA_DV:(h + 1) * GLA_DV],
                         preferred_element_type=F32) for h in range(GLA_HEADS)], axis=0)
            s_ref[...] = dec * s_prev + u

            gate = r_ref[lo:lo + C, :].astype(F32)
            for h in range(GLA_HEADS):
                hs = slice(h * GLA_DV, (h + 1) * GLA_DV)
                oh = _rms(o[:, hs], gn_ref[:, hs])
                o_ref[lo:lo + C, hs] = (oh * gate[:, hs]).astype(BF16)


def _gla(gq, gk, gv, la, gr, gn, ltri, late, batch):
    T = gq.shape[0]
    tiles = T // batch // GLA_TILE
    steps = batch * tiles
    row = lambda n: pl.BlockSpec((GLA_TILE, n), lambda b, t: (b * tiles + t, 0))
    for w in late:
        assert w.shape[0] % (steps * BF16_SUBLANES) == 0
    late_specs = [pl.BlockSpec((w.shape[0] // steps, w.shape[1]), lambda b, t: (b * tiles + t, 0))
                  for w in late]
    res = pl.pallas_call(
        _gla_kernel, grid=(batch, tiles),
        in_specs=[row(GLA_KW), row(GLA_KW), row(GLA_WIDTH), row(GLA_KW), row(GLA_WIDTH),
                  _const_spec(gn.shape), _const_spec(ltri.shape)] + late_specs,
        out_specs=[row(GLA_WIDTH)] + late_specs,
        out_shape=[jax.ShapeDtypeStruct((T, GLA_WIDTH), BF16)]
        + [jax.ShapeDtypeStruct(w.shape, BF16) for w in late],
        scratch_shapes=[pltpu.VMEM((GLA_KW, GLA_DV), F32)],
        compiler_params=pltpu.CompilerParams(dimension_semantics=("arbitrary", "arbitrary"),
                                             vmem_limit_bytes=VMEM_LIMIT_BYTES),
        name="gla")(gq, gk, gv, la, gr, gn, ltri, *late)
    return res[0], res[1:]


def _attn_kernel(q_ref, k_ref, v_ref, o_ref, xf_ref, *scratch):
    B = ATT_BLOCK
    T = ATT_SUPER
    nd = len(ATT_DILATIONS)
    qd = dict(zip(ATT_DILATIONS[1:], scratch[:nd - 1]))
    kc = dict(zip(ATT_DILATIONS, scratch[nd - 1:2 * nd - 1]))
    vc = dict(zip(ATT_DILATIONS, scratch[2 * nd - 1:3 * nd - 1]))
    acc = scratch[3 * nd - 1:]
    acc_o = dict(zip(ATT_DILATIONS, acc[0::2]))
    acc_l = dict(zip(ATT_DILATIONS, acc[1::2]))
    tile = pl.program_id(2)

    def stage(src_ref, dst, halo):
        xf_ref[...] = src_ref[...].astype(F32)
        for d in ATT_DILATIONS[1:]:
            n = T // d
            for r in range(d):
                lo_row = r * (n + halo) + halo
                dst[d][lo_row:lo_row + n, :] = xf_ref[pl.ds(r, n, stride=d), :].astype(BF16)

    @pl.when(tile == 0)
    def _():
        zeros = jnp.zeros((B, LANES), BF16)
        for d in ATT_DILATIONS:
            for r in range(d):
                row = r * (T // d + B)
                kc[d][row:row + B, :] = zeros
                vc[d][row:row + B, :] = zeros

    stage(q_ref, qd, 0)
    stage(k_ref, kc, B)
    stage(v_ref, vc, B)
    kc[1][B:, :] = k_ref[...]
    vc[1][B:, :] = v_ref[...]

    qi = lax.broadcasted_iota(jnp.int32, (B, 2 * B), 0)
    ki = lax.broadcasted_iota(jnp.int32, (B, 2 * B), 1)
    band = (ki >= qi) & (ki <= qi + B)
    bias_band = jnp.where(band, 0.0, NEG).astype(F32)
    bias_first = jnp.where(band & ((ki >= B) | (tile > 0)), 0.0, NEG).astype(F32)
    lane = lax.broadcasted_iota(jnp.int32, (B, LANES), 1)
    lo = lane < ATT_HD
    zero_bf = jnp.zeros((), BF16)
    ones_v = jnp.ones((2 * B, LANES), BF16)

    def block(q_src, q_row, d, k_row, bias, rows_out):
        qp = q_src[q_row:q_row + B, :]
        kk = kc[d][k_row:k_row + 2 * B, :]
        vv = vc[d][k_row:k_row + 2 * B, :]
        qs = jnp.concatenate([jnp.where(lo, qp, zero_bf), jnp.where(lo, zero_bf, qp)], axis=0)
        s = lax.dot_general(qs, kk, (((1,), (1,)), ((), ())), preferred_element_type=F32)
        s = s + jnp.concatenate([bias, bias], axis=0)
        m = jnp.max(s, axis=-1, keepdims=True)
        p = jnp.exp2(s - m).astype(BF16)
        pv = jnp.dot(p, jnp.concatenate([vv, ones_v], axis=1), preferred_element_type=F32)
        den = jnp.where(lo, pv[:B, LANES:], pv[B:, LANES:])
        acc_o[d][rows_out, :] = jnp.where(lo, pv[:B, :LANES], pv[B:, :LANES]) / den
        acc_l[d][rows_out, :] = jnp.where(lo, m[:B], m[B:]) + jnp.log2(den)

    for d in ATT_DILATIONS:
        n = T // d
        for r in range(d):
            for jb in range(n // B):
                q_src, q_row = (q_ref, jb * B) if d == 1 else (qd[d], r * n + jb * B)
                rows_out = pl.ds(r + d * B * jb, B, stride=d) if d > 1 else pl.ds(jb * B, B)
                block(q_src, q_row, d, r * (n + B) + jb * B, bias_first if jb == 0 else bias_band,
                      rows_out)

    for c in range(T // ATT_COMBINE_ROWS):
        rows = slice(c * ATT_COMBINE_ROWS, (c + 1) * ATT_COMBINE_ROWS)
        ls = [acc_l[d][rows, :] for d in ATT_DILATIONS]
        mx = functools.reduce(jnp.maximum, ls)
        es = [jnp.exp2(l - mx) for l in ls]
        num = functools.reduce(jnp.add, [e * acc_o[d][rows, :] for e, d in zip(es, ATT_DILATIONS)])
        o_ref[rows, :] = (num / functools.reduce(jnp.add, es)).astype(o_ref.dtype)

    for d in ATT_DILATIONS:
        n = T // d
        for r in range(d):
            row = r * (n + B)
            kc[d][row:row + B, :] = kc[d][row + n:row + n + B, :]
            vc[d][row:row + B, :] = vc[d][row + n:row + n + B, :]


def _attn(q, k, v, batch, seq):
    T = q.shape[0]
    tiles = seq // ATT_SUPER
    spec = pl.BlockSpec((ATT_SUPER, LANES), lambda b, h, t: (b * tiles + t, h))
    rows = lambda n, dt: pltpu.VMEM((n, LANES), dt)
    kv_rows = [ATT_SUPER + d * ATT_BLOCK for d in ATT_DILATIONS]
    scratch = ([rows(ATT_SUPER, F32)] + [rows(ATT_SUPER, BF16) for _ in ATT_DILATIONS[1:]]
               + [rows(n, BF16) for n in kv_rows] + [rows(n, BF16) for n in kv_rows]
               + [rows(ATT_SUPER, F32) for _ in range(2 * len(ATT_DILATIONS))])
    return pl.pallas_call(
        _attn_kernel, grid=(batch, ATT_WIDTH // LANES, tiles), in_specs=[spec, spec, spec],
        out_specs=spec, out_shape=jax.ShapeDtypeStruct((T, ATT_WIDTH), BF16), scratch_shapes=scratch,
        compiler_params=pltpu.CompilerParams(
            dimension_semantics=("arbitrary", "arbitrary", "arbitrary"),
            vmem_limit_bytes=VMEM_LIMIT_BYTES),
        name="attn")(q, k, v)


def _out_ffn_kernel(x1_ref, og_ref, oa_ref, ga_ref, wo_ref, g2_ref, w1_ref, w3_ref, w2_ref, gf_ref,
                    out_ref):
    groups = [slice(r, r + ROW_GROUP) for r in range(0, OUT_TOKEN_TILE, ROW_GROUP)]
    x2s = []
    for rows in groups:
        oa = _rms(oa_ref[rows, :].astype(F32), ga_ref[...]).astype(BF16)
        x2s.append(x1_ref[rows, :]
                   + jnp.dot(og_ref[rows, :], wo_ref[:GLA_WIDTH, :], preferred_element_type=F32)
                   + jnp.dot(oa, wo_ref[GLA_WIDTH:, :], preferred_element_type=F32))
    for rows, x2 in zip(groups, x2s):
        h = _rms(x2, g2_ref[...]).astype(BF16)
        x3 = x2 + 0.5 * _swiglu_half_step(h, w1_ref, w3_ref, w2_ref)
        out_ref[rows, :] = _rms(x3, gf_ref[...])


def _out_ffn(x1, og, oa, ga, wo, g2, w1, w3, w2, gf):
    T = x1.shape[0]
    tm = OUT_TOKEN_TILE
    row = lambda n: pl.BlockSpec((tm, n), lambda i: (i, 0))
    return pl.pallas_call(
        _out_ffn_kernel, grid=(T // tm,),
        in_specs=[row(D_MODEL), row(GLA_WIDTH), row(ATT_WIDTH), _const_spec(ga.shape),
                  _const_spec(wo.shape), _const_spec(g2.shape),
                  _const_spec(w1.shape), _const_spec(w3.shape), _const_spec(w2.shape),
                  _const_spec(gf.shape)],
        out_specs=row(D_MODEL),
        out_shape=jax.ShapeDtypeStruct((T, D_MODEL), F32),
        compiler_params=pltpu.CompilerParams(dimension_semantics=("arbitrary",),
                                             vmem_limit_bytes=VMEM_LIMIT_BYTES),
        name="out_ffn")(x1, og, oa, ga, wo, g2, w1, w3, w2, gf)


def _rope_tables():
    j = np.arange(LANES) % ATT_HD
    half = ROT_DIM // 2
    inv_freq = ROPE_THETA ** (-np.arange(0, ROT_DIM, 2, dtype=np.float32) / ROT_DIM)
    freq = np.where(j < ROT_DIM, inv_freq[j % half], 0.0).astype(np.float32)
    sgn1 = np.where(j < half, -1.0, 0.0).astype(np.float32)
    sgn2 = np.where((j >= half) & (j < ROT_DIM), 1.0, 0.0).astype(np.float32)
    return freq[None, :], sgn1[None, :], sgn2[None, :]


def _chunk_tril():
    i = np.arange(GLA_HALF)
    same = (i[:, None] // GLA_CHUNK) == (i[None, :] // GLA_CHUNK)
    return jnp.asarray((same & (i[None, :] <= i[:, None])).astype(np.float32), dtype=BF16)


def kernel(x, positions, ffn1_norm, ffn1_w1, ffn1_w3, ffn1_w2, mix_norm, w_in, gla_w_a2, gla_b_a,
           gla_out_norm, att_out_norm, w_out, ffn2_norm, ffn2_w1, ffn2_w3, ffn2_w2, final_norm):
    batch, seq, _ = x.shape
    T = batch * seq
    depth = ffn1_norm.shape[0]
    assert depth == 1, "the final norm is fused into the single layer's last kernel"
    freq, sgn1, sgn2 = (jnp.asarray(t) for t in _rope_tables())
    ltri = _chunk_tril()
    pos = positions.astype(F32).reshape(T // TOKEN_TILE, 1, TOKEN_TILE)
    xs = x.reshape(T, D_MODEL)

    o_gla_end = 2 * GLA_KW + 2 * GLA_WIDTH
    o_ga_end = o_gla_end + GLA_RANK
    for l in range(depth):
        wi = w_in[l]
        wgla = wi[:, :o_gla_end].astype(BF16)
        wga = jnp.pad(wi[:, o_gla_end:o_ga_end].astype(BF16), ((0, 0), (0, LANES - GLA_RANK)))
        watt = wi[:, o_ga_end:].astype(BF16)
        wa2 = jnp.pad(gla_w_a2[l].astype(BF16), ((0, LANES - GLA_RANK), (0, 0)))
        x1, gq, gk, gv, gr, la, aq, ak, av = _ffn_proj(
            xs, pos, ffn1_norm[l][None, :], ffn1_w1[l].astype(BF16), ffn1_w3[l].astype(BF16),
            ffn1_w2[l].astype(BF16), mix_norm[l][None, :], wgla, wga, wa2, gla_b_a[l][None, :], watt,
            freq, sgn1, sgn2)

        o_gla, (w1b, w3b, w2b, wob) = _gla(
            gq, gk, gv, la, gr, gla_out_norm[l][None, :], ltri,
            late=(ffn2_w1[l], ffn2_w3[l], ffn2_w2[l], w_out[l]), batch=batch)

        o_att = _attn(aq, ak, av, batch, seq)

        xs = _out_ffn(x1, o_gla, o_att, att_out_norm[l][None, :], wob, ffn2_norm[l][None, :], w1b, w3b,
                      w2b, final_norm[None, :])
    return xs.reshape(batch, seq, D_MODEL)
```

```python
import functools
import math

import jax
import jax.numpy as jnp
import numpy as np
from jax import lax
from jax.experimental import pallas as pl
from jax.experimental.pallas import tpu as pltpu

F32 = jnp.float32
BF16 = jnp.bfloat16

D_MODEL = 1024
D_FF = 2816
GLA_WIDTH = 512
GLA_HEADS = 4
GLA_DV = 128
GLA_DK = 64
GLA_KW = GLA_HEADS * GLA_DK
GLA_RANK = 16
GLA_TAU = 16.0
GLA_CHUNK = 64
ATT_WIDTH = 512
ATT_HEADS = 8
ATT_HD = 64
ROT_DIM = 16
ROPE_THETA = 500000.0
DILATED_PATTERNS = ((128, 1), (512, 4), (2048, 16))
ATT_BLOCK = 128
EPS = 1e-6

LANES = 128
SUBLANES = 8
BF16_SUBLANES = 16
VMEM_LIMIT_BYTES = 56 * 1024 * 1024
FFN_PROJ_VMEM_LIMIT_BYTES = 60 * 1024 * 1024

TOKEN_TILE = 1024
OUT_TOKEN_TILE = 1024
ROW_GROUP = 256
MXU_TILE = 256
FF_SPLITS = (0, (D_FF // MXU_TILE // 2) * MXU_TILE, D_FF)
assert D_FF % MXU_TILE == 0
GLA_TILE = 2048
GLA_HALF = 256
ATT_DILATIONS = tuple(sorted(d for _, d in DILATED_PATTERNS))
ATT_SUPER = max(ATT_DILATIONS) * ATT_BLOCK
ATT_COMBINE_ROWS = 256
NEG = -1e30
LOG2E = math.log2(math.e)

assert all(w // d == ATT_BLOCK for w, d in DILATED_PATTERNS)
assert ATT_DILATIONS[0] == 1 and all(ATT_SUPER % (d * ATT_BLOCK) == 0 for d in ATT_DILATIONS)
assert ATT_DILATIONS == (1, 4, 16)


def _const_spec(shape):
    nd = len(shape)
    return pl.BlockSpec(shape, lambda *_: (0,) * nd, pipeline_mode=pl.Buffered(1))


def _rms(x, g):
    return x * lax.rsqrt(jnp.mean(x * x, axis=-1, keepdims=True) + EPS) * g


def _swiglu_half_step(h, w1_ref, w3_ref, w2_ref):
    acc = None
    for lo, hi in zip(FF_SPLITS[:-1], FF_SPLITS[1:]):
        sl = slice(lo, hi)
        a = jnp.dot(h, w1_ref[:, sl], preferred_element_type=F32)
        b = jnp.dot(h, w3_ref[:, sl], preferred_element_type=F32)
        g = (a / (1.0 + jnp.exp(-a)) * b).astype(BF16)
        part = jnp.dot(g, w2_ref[sl, :], preferred_element_type=F32)
        acc = part if acc is None else acc + part
    return acc


def _ffn_proj_kernel(x_ref, pos_ref, g1_ref, w1_ref, w3_ref, w2_ref, gmix_ref, wgla_ref, wga_ref,
                     wa2_ref, ba_ref, watt_ref, freq_ref, sgn1_ref, sgn2_ref,
                     x1_ref, gq_ref, gk_ref, gv_ref, gr_ref, la_ref, aq_ref, ak_ref, av_ref):
    groups = [slice(r, r + ROW_GROUP) for r in range(0, TOKEN_TILE, ROW_GROUP)]
    x1s = []
    for rows in groups:
        x = x_ref[rows, :]
        h = _rms(x, g1_ref[...]).astype(BF16)
        x1 = x + 0.5 * _swiglu_half_step(h, w1_ref, w3_ref, w2_ref)
        x1_ref[rows, :] = x1
        x1s.append(x1)

    pos_col = jnp.broadcast_to(pos_ref[...], (SUBLANES, TOKEN_TILE)).T[:, 0:1]
    for rows, x1 in zip(groups, x1s):
        h2 = _rms(x1, gmix_ref[...]).astype(BF16)
        pa = jnp.dot(h2, watt_ref[...], preferred_element_type=F32)
        av_ref[rows, :] = pa[:, 2 * ATT_WIDTH:].astype(BF16)

        ang = pos_col[rows, :] * freq_ref[...]
        cos = jnp.cos(ang)
        sin = jnp.sin(ang)
        s1 = sin * sgn1_ref[...]
        s2 = sin * sgn2_ref[...]
        half = ROT_DIM // 2
        for off, ref, scale in ((0, aq_ref, ATT_HD ** -0.5 * LOG2E), (ATT_WIDTH, ak_ref, 1.0)):
            for cg in range(ATT_WIDTH // LANES):
                t = pa[:, off + cg * LANES: off + (cg + 1) * LANES]
                r = t * cos + pltpu.roll(t, LANES - half, 1) * s1 + pltpu.roll(t, half, 1) * s2
                ref[rows, cg * LANES:(cg + 1) * LANES] = (r * scale).astype(BF16)

        ga = jnp.dot(h2, wga_ref[...], preferred_element_type=F32).astype(BF16)
        z = jnp.dot(ga, wa2_ref[...], preferred_element_type=F32) + ba_ref[...]
        la_ref[rows, :] = (jnp.minimum(z, 0.0) - jnp.log(1.0 + jnp.exp(-jnp.abs(z)))) * (1.0 / GLA_TAU)

        pg = jnp.dot(h2, wgla_ref[...], preferred_element_type=F32)
        gq_ref[rows, :] = (pg[:, :GLA_KW] * (GLA_DK ** -0.5)).astype(BF16)
        gk_ref[rows, :] = pg[:, GLA_KW:2 * GLA_KW].astype(BF16)
        gv_ref[rows, :] = pg[:, 2 * GLA_KW:2 * GLA_KW + GLA_WIDTH].astype(BF16)
        gr = pg[:, 2 * GLA_KW + GLA_WIDTH:]
        gr_ref[rows, :] = (gr / (1.0 + jnp.exp(-gr))).astype(BF16)


def _ffn_proj(x2d, pos, g1, w1, w3, w2, gmix, wgla, wga, wa2, ba, watt, freq, sgn1, sgn2):
    T = x2d.shape[0]
    tm = TOKEN_TILE
    row = lambda n: pl.BlockSpec((tm, n), lambda i: (i, 0))
    pos_spec = pl.BlockSpec((None, 1, tm), lambda i: (i, 0, 0))
    in_specs = [row(D_MODEL), pos_spec, _const_spec(g1.shape), _const_spec(w1.shape), _const_spec(w3.shape),
                _const_spec(w2.shape), _const_spec(gmix.shape), _const_spec(wgla.shape),
                _const_spec(wga.shape), _const_spec(wa2.shape), _const_spec(ba.shape),
                _const_spec(watt.shape), _const_spec(freq.shape), _const_spec(sgn1.shape),
                _const_spec(sgn2.shape)]
    out_shape = [jax.ShapeDtypeStruct((T, D_MODEL), F32),
                 jax.ShapeDtypeStruct((T, GLA_KW), BF16), jax.ShapeDtypeStruct((T, GLA_KW), BF16),
                 jax.ShapeDtypeStruct((T, GLA_WIDTH), BF16), jax.ShapeDtypeStruct((T, GLA_WIDTH), BF16),
                 jax.ShapeDtypeStruct((T, GLA_KW), F32),
                 jax.ShapeDtypeStruct((T, ATT_WIDTH), BF16), jax.ShapeDtypeStruct((T, ATT_WIDTH), BF16),
                 jax.ShapeDtypeStruct((T, ATT_WIDTH), BF16)]
    return pl.pallas_call(
        _ffn_proj_kernel, grid=(T // tm,), in_specs=in_specs,
        out_specs=[row(s.shape[1]) for s in out_shape], out_shape=out_shape,
        compiler_params=pltpu.CompilerParams(dimension_semantics=("arbitrary",),
                                             vmem_limit_bytes=FFN_PROJ_VMEM_LIMIT_BYTES),
        name="ffn_proj")(x2d, pos, g1, w1, w3, w2, gmix, wgla, wga, wa2, ba, watt, freq, sgn1, sgn2)


def _gla_kernel(q_ref, k_ref, v_ref, la_ref, r_ref, gn_ref, ltri_ref, o_ref, s_ref):
    @pl.when(pl.program_id(1) == 0)
    def _():
        s_ref[...] = jnp.zeros_like(s_ref)

    C = GLA_CHUNK
    r_kk = lax.broadcasted_iota(jnp.int32, (GLA_KW, GLA_KW), 0) // C
    c_kk = lax.broadcasted_iota(jnp.int32, (GLA_KW, GLA_KW), 1) // C
    bd_k = r_kk == c_kk
    r_kv = lax.broadcasted_iota(jnp.int32, (GLA_KW, GLA_WIDTH), 0) // C
    c_kv = lax.broadcasted_iota(jnp.int32, (GLA_KW, GLA_WIDTH), 1) // GLA_DV
    bd_v = r_kv == c_kv
    causal = (lax.broadcasted_iota(jnp.int32, (C, GLA_KW), 1) % C
              <= lax.broadcasted_iota(jnp.int32, (C, GLA_KW), 0))
    ltri = ltri_ref[...]

    for hf in range(GLA_TILE // GLA_HALF):
        base = hf * GLA_HALF
        g = la_ref[base:base + GLA_HALF, :]
        g_hi = g.astype(BF16)
        g_lo = (g - g_hi.astype(F32)).astype(BF16)
        b = (jnp.dot(ltri, g_hi, preferred_element_type=F32)
             + jnp.dot(ltri, g_lo, preferred_element_type=F32))
        for c in range(GLA_HALF // C):
            lo = base + c * C
            bc = b[c * C:(c + 1) * C, :]
            bl = bc[C - 1:C, :]
            qc = q_ref[lo:lo + C, :].astype(F32)
            kc = k_ref[lo:lo + C, :].astype(F32)
            vc = v_ref[lo:lo + C, :]
            qd = (qc * jnp.exp(bc)).astype(BF16)
            k_inv = kc * jnp.exp(-bc)
            ki = k_inv.astype(BF16)
            kt = k_inv * jnp.exp(bl)
            kbd = jnp.where(bd_k, jnp.tile(ki, (GLA_HEADS, 1)), jnp.zeros((), BF16))
            a = lax.dot_general(qd, kbd, (((1,), (1,)), ((), ())), preferred_element_type=F32)
            a = jnp.where(causal, a, 0.0).astype(BF16)
            vbd = jnp.where(bd_v, jnp.tile(vc, (GLA_HEADS, 1)), jnp.zeros((), BF16))
            s_prev = s_ref[...]
            sbd = jnp.where(bd_v, jnp.tile(s_prev.astype(BF16), (1, GLA_HEADS)), jnp.zeros((), BF16))
            o = (jnp.dot(a, vbd, preferred_element_type=F32)
                 + jnp.dot(qd, sbd, preferred_element_type=F32))

            kt_t = kt.T.astype(BF16)
            dec = jnp.exp(bc.T[:, C - 1:C])
            u = jnp.concatenate(
                [jnp.dot(kt_t[h * GLA_DK:(h + 1) * GLA_DK, :], vc[:, h * GLA_DV:(h + 1) * GLA_DV],
                         preferred_element_type=F32) for h in range(GLA_HEADS)], axis=0)
            s_ref[...] = dec * s_prev + u

            gate = r_ref[lo:lo + C, :].astype(F32)
            for h in range(GLA_HEADS):
                hs = slice(h * GLA_DV, (h + 1) * GLA_DV)
                oh = _rms(o[:, hs], gn_ref[:, hs])
                o_ref[lo:lo + C, hs] = (oh * gate[:, hs]).astype(BF16)


def _gla(gq, gk, gv, la, gr, gn, ltri, batch):
    T = gq.shape[0]
    tiles = T // batch // GLA_TILE
    row = lambda n: pl.BlockSpec((GLA_TILE, n), lambda b, t: (b * tiles + t, 0))
    return pl.pallas_call(
        _gla_kernel, grid=(batch, tiles),
        in_specs=[row(GLA_KW), row(GLA_KW), row(GLA_WIDTH), row(GLA_KW), row(GLA_WIDTH),
                  _const_spec(gn.shape), _const_spec(ltri.shape)],
        out_specs=row(GLA_WIDTH),
        out_shape=jax.ShapeDtypeStruct((T, GLA_WIDTH), BF16),
        scratch_shapes=[pltpu.VMEM((GLA_KW, GLA_DV), F32)],
        compiler_params=pltpu.CompilerParams(dimension_semantics=("arbitrary", "arbitrary"),
                                             vmem_limit_bytes=VMEM_LIMIT_BYTES),
        name="gla")(gq, gk, gv, la, gr, gn, ltri)


def _attn_kernel(n_late, q_ref, k_ref, v_ref, *rest):
    late_in, o_ref, late_out = rest[:n_late], rest[n_late], rest[n_late + 1:2 * n_late + 1]
    xf_ref, x4_ref, *scratch = rest[2 * n_late + 1:]
    for src, dst in zip(late_in, late_out):
        dst[...] = src[...].astype(BF16)

    B = ATT_BLOCK
    T = ATT_SUPER
    nd = len(ATT_DILATIONS)
    qd = dict(zip(ATT_DILATIONS[1:], scratch[:nd - 1]))
    kc = dict(zip(ATT_DILATIONS, scratch[nd - 1:2 * nd - 1]))
    vc = dict(zip(ATT_DILATIONS, scratch[2 * nd - 1:3 * nd - 1]))
    acc = scratch[3 * nd - 1:]
    acc_o = dict(zip(ATT_DILATIONS, acc[0::2]))
    acc_l = dict(zip(ATT_DILATIONS, acc[1::2]))
    tile = pl.program_id(2)

    def stage(src_ref, dst, halo):
        xf_ref[...] = src_ref[...].astype(F32)
        n4 = T // 4
        for r in range(4):
            c4 = xf_ref[pl.ds(r, n4, stride=4), :]
            x4_ref[r * n4:(r + 1) * n4, :] = c4
            lo_row = r * (n4 + halo) + halo
            dst[4][lo_row:lo_row + n4, :] = c4.astype(BF16)
        n16 = T // 16
        for r in range(16):
            lo_row = r * (n16 + halo) + halo
            dst[16][lo_row:lo_row + n16, :] = (
                x4_ref[pl.ds((r % 4) * n4 + r // 4, n16, stride=4), :].astype(BF16))

    @pl.when(tile == 0)
    def _():
        zeros = jnp.zeros((B, LANES), BF16)
        for d in ATT_DILATIONS:
            for r in range(d):
                row = r * (T // d + B)
                kc[d][row:row + B, :] = zeros
                vc[d][row:row + B, :] = zeros

    stage(q_ref, qd, 0)
    stage(k_ref, kc, B)
    stage(v_ref, vc, B)
    kc[1][B:, :] = k_ref[...]
    vc[1][B:, :] = v_ref[...]

    qi = lax.broadcasted_iota(jnp.int32, (B, 2 * B), 0)
    ki = lax.broadcasted_iota(jnp.int32, (B, 2 * B), 1)
    band = (ki >= qi) & (ki <= qi + B)
    bias_band = jnp.where(band, 0.0, NEG).astype(F32)
    bias_first = jnp.where(band & ((ki >= B) | (tile > 0)), 0.0, NEG).astype(F32)
    lane = lax.broadcasted_iota(jnp.int32, (B, LANES), 1)
    lo = lane < ATT_HD
    zero_bf = jnp.zeros((), BF16)
    ones_v = jnp.ones((2 * B, LANES), BF16)

    def block(q_src, q_row, d, k_row, bias, rows_out):
        qp = q_src[q_row:q_row + B, :]
        kk = kc[d][k_row:k_row + 2 * B, :]
        vv = vc[d][k_row:k_row + 2 * B, :]
        qs = jnp.concatenate([jnp.where(lo, qp, zero_bf), jnp.where(lo, zero_bf, qp)], axis=0)
        s = lax.dot_general(qs, kk, (((1,), (1,)), ((), ())), preferred_element_type=F32)
        s = s + jnp.concatenate([bias, bias], axis=0)
        m = jnp.max(s, axis=-1, keepdims=True)
        p = jnp.exp2(s - m).astype(BF16)
        pv = jnp.dot(p, jnp.concatenate([vv, ones_v], axis=1), preferred_element_type=F32)
        den = jnp.where(lo, pv[:B, LANES:], pv[B:, LANES:])
        acc_o[d][rows_out, :] = jnp.where(lo, pv[:B, :LANES], pv[B:, :LANES]) / den
        acc_l[d][rows_out, :] = jnp.where(lo, m[:B], m[B:]) + jnp.log2(den)

    for d in ATT_DILATIONS:
        n = T // d
        for r in range(d):
            for jb in range(n // B):
                q_src, q_row = (q_ref, jb * B) if d == 1 else (qd[d], r * n + jb * B)
                rows_out = pl.ds(r + d * B * jb, B, stride=d) if d > 1 else pl.ds(jb * B, B)
                block(q_src, q_row, d, r * (n + B) + jb * B, bias_first if jb == 0 else bias_band,
                      rows_out)

    for c in range(T // ATT_COMBINE_ROWS):
        rows = slice(c * ATT_COMBINE_ROWS, (c + 1) * ATT_COMBINE_ROWS)
        ls = [acc_l[d][rows, :] for d in ATT_DILATIONS]
        mx = functools.reduce(jnp.maximum, ls)
        es = [jnp.exp2(l - mx) for l in ls]
        num = functools.reduce(jnp.add, [e * acc_o[d][rows, :] for e, d in zip(es, ATT_DILATIONS)])
        o_ref[rows, :] = (num / functools.reduce(jnp.add, es)).astype(o_ref.dtype)

    for d in ATT_DILATIONS:
        n = T // d
        for r in range(d):
            row = r * (n + B)
            kc[d][row:row + B, :] = kc[d][row + n:row + n + B, :]
            vc[d][row:row + B, :] = vc[d][row + n:row + n + B, :]


def _late_chunk_spec(shape, steps, step_index):
    rows, cols = shape
    share = 1 if (rows // steps) % BF16_SUBLANES == 0 else 2
    assert rows % (steps // share) == 0 and (rows * share // steps) % BF16_SUBLANES == 0
    return pl.BlockSpec((rows * share // steps, cols), lambda *idx: (step_index(*idx) // share, 0))


def _attn(q, k, v, late, batch, seq):
    T = q.shape[0]
    tiles = seq // ATT_SUPER
    pairs = ATT_WIDTH // LANES
    spec = pl.BlockSpec((ATT_SUPER, LANES), lambda b, h, t: (b * tiles + t, h))
    late_specs = [_late_chunk_spec(w.shape, batch * pairs * tiles,
                                   lambda b, h, t: (b * pairs + h) * tiles + t) for w in late]
    rows = lambda n, dt: pltpu.VMEM((n, LANES), dt)
    kv_rows = [ATT_SUPER + d * ATT_BLOCK for d in ATT_DILATIONS]
    scratch = ([rows(ATT_SUPER, F32)] * 2 + [rows(ATT_SUPER, BF16) for _ in ATT_DILATIONS[1:]]
               + [rows(n, BF16) for n in kv_rows] + [rows(n, BF16) for n in kv_rows]
               + [rows(ATT_SUPER, F32) for _ in range(2 * len(ATT_DILATIONS))])
    res = pl.pallas_call(
        functools.partial(_attn_kernel, len(late)), grid=(batch, pairs, tiles),
        in_specs=[spec, spec, spec] + late_specs, out_specs=[spec] + late_specs,
        out_shape=[jax.ShapeDtypeStruct((T, ATT_WIDTH), BF16)]
        + [jax.ShapeDtypeStruct(w.shape, BF16) for w in late],
        scratch_shapes=scratch,
        compiler_params=pltpu.CompilerParams(
            dimension_semantics=("arbitrary", "arbitrary", "arbitrary"),
            vmem_limit_bytes=VMEM_LIMIT_BYTES),
        name="attn")(q, k, v, *late)
    return res[0], res[1:]


def _out_ffn_kernel(x1_ref, og_ref, oa_ref, ga_ref, wo_ref, g2_ref, w1_ref, w3_ref, w2_ref, gf_ref,
                    out_ref):
    groups = [slice(r, r + ROW_GROUP) for r in range(0, OUT_TOKEN_TILE, ROW_GROUP)]
    x2s = []
    for rows in groups:
        oa = _rms(oa_ref[rows, :].astype(F32), ga_ref[...]).astype(BF16)
        x2s.append(x1_ref[rows, :]
                   + jnp.dot(og_ref[rows, :], wo_ref[:GLA_WIDTH, :], preferred_element_type=F32)
                   + jnp.dot(oa, wo_ref[GLA_WIDTH:, :], preferred_element_type=F32))
    for rows, x2 in zip(groups, x2s):
        h = _rms(x2, g2_ref[...]).astype(BF16)
        x3 = x2 + 0.5 * _swiglu_half_step(h, w1_ref, w3_ref, w2_ref)
        out_ref[rows, :] = _rms(x3, gf_ref[...])


def _out_ffn(x1, og, oa, ga, wo, g2, w1, w3, w2, gf):
    T = x1.shape[0]
    tm = OUT_TOKEN_TILE
    row = lambda n: pl.BlockSpec((tm, n), lambda i: (i, 0))
    return pl.pallas_call(
        _out_ffn_kernel, grid=(T // tm,),
        in_specs=[row(D_MODEL), row(GLA_WIDTH), row(ATT_WIDTH), _const_spec(ga.shape),
                  _const_spec(wo.shape), _const_spec(g2.shape),
                  _const_spec(w1.shape), _const_spec(w3.shape), _const_spec(w2.shape),
                  _const_spec(gf.shape)],
        out_specs=row(D_MODEL),
        out_shape=jax.ShapeDtypeStruct((T, D_MODEL), F32),
        compiler_params=pltpu.CompilerParams(dimension_semantics=("arbitrary",),
                                             vmem_limit_bytes=VMEM_LIMIT_BYTES),
        name="out_ffn")(x1, og, oa, ga, wo, g2, w1, w3, w2, gf)


def _rope_tables():
    j = np.arange(LANES) % ATT_HD
    half = ROT_DIM // 2
    inv_freq = ROPE_THETA ** (-np.arange(0, ROT_DIM, 2, dtype=np.float32) / ROT_DIM)
    freq = np.where(j < ROT_DIM, inv_freq[j % half], 0.0).astype(np.float32)
    sgn1 = np.where(j < half, -1.0, 0.0).astype(np.float32)
    sgn2 = np.where((j >= half) & (j < ROT_DIM), 1.0, 0.0).astype(np.float32)
    return freq[None, :], sgn1[None, :], sgn2[None, :]


def _chunk_tril():
    i = np.arange(GLA_HALF)
    same = (i[:, None] // GLA_CHUNK) == (i[None, :] // GLA_CHUNK)
    return jnp.asarray((same & (i[None, :] <= i[:, None])).astype(np.float32), dtype=BF16)


def kernel(x, positions, ffn1_norm, ffn1_w1, ffn1_w3, ffn1_w2, mix_norm, w_in, gla_w_a2, gla_b_a,
           gla_out_norm, att_out_norm, w_out, ffn2_norm, ffn2_w1, ffn2_w3, ffn2_w2, final_norm):
    batch, seq, _ = x.shape
    T = batch * seq
    depth = ffn1_norm.shape[0]
    assert depth == 1, "the final norm is fused into the single layer's last kernel"
    freq, sgn1, sgn2 = (jnp.asarray(t) for t in _rope_tables())
    ltri = _chunk_tril()
    pos = positions.astype(F32).reshape(T // TOKEN_TILE, 1, TOKEN_TILE)
    xs = x.reshape(T, D_MODEL)

    o_gla_end = 2 * GLA_KW + 2 * GLA_WIDTH
    o_ga_end = o_gla_end + GLA_RANK
    for l in range(depth):
        wi = w_in[l]
        wgla = wi[:, :o_gla_end].astype(BF16)
        wga = jnp.pad(wi[:, o_gla_end:o_ga_end].astype(BF16), ((0, 0), (0, LANES - GLA_RANK)))
        watt = wi[:, o_ga_end:].astype(BF16)
        wa2 = jnp.pad(gla_w_a2[l].astype(BF16), ((0, LANES - GLA_RANK), (0, 0)))
        x1, gq, gk, gv, gr, la, aq, ak, av = _ffn_proj(
            xs, pos, ffn1_norm[l][None, :], ffn1_w1[l].astype(BF16), ffn1_w3[l].astype(BF16),
            ffn1_w2[l].astype(BF16), mix_norm[l][None, :], wgla, wga, wa2, gla_b_a[l][None, :], watt,
            freq, sgn1, sgn2)

        o_gla = _gla(gq, gk, gv, la, gr, gla_out_norm[l][None, :], ltri, batch)

        o_att, (w1b, w3b, w2b, wob) = _attn(
            aq, ak, av, (ffn2_w1[l], ffn2_w3[l], ffn2_w2[l], w_out[l]), batch, seq)

        xs = _out_ffn(x1, o_gla, o_att, att_out_norm[l][None, :], wob, ffn2_norm[l][None, :], w1b, w3b,
                      w2b, final_norm[None, :])
    return xs.reshape(batch, seq, D_MODEL)
```

```python
import functools
import math

import jax
import jax.numpy as jnp
import numpy as np
from jax import lax
from jax.experimental import pallas as pl
from jax.experimental.pallas import tpu as pltpu

F32 = jnp.float32
BF16 = jnp.bfloat16

D_MODEL = 1024
D_FF = 2816
GLA_WIDTH = 512
GLA_HEADS = 4
GLA_DV = 128
GLA_DK = 64
GLA_KW = GLA_HEADS * GLA_DK
GLA_RANK = 16
GLA_TAU = 16.0
GLA_CHUNK = 64
ATT_WIDTH = 512
ATT_HEADS = 8
ATT_HD = 64
ROT_DIM = 16
ROPE_THETA = 500000.0
DILATED_PATTERNS = ((128, 1), (512, 4), (2048, 16))
ATT_BLOCK = 128
EPS = 1e-6

LANES = 128
SUBLANES = 8
BF16_SUBLANES = 16
VMEM_LIMIT_BYTES = 56 * 1024 * 1024
FFN_PROJ_VMEM_LIMIT_BYTES = 60 * 1024 * 1024

TOKEN_TILE = 1024
OUT_TOKEN_TILE = 1024
ROW_GROUP = 256
MXU_TILE = 256
FF_SPLITS = (0, (D_FF // MXU_TILE // 2) * MXU_TILE, D_FF)
assert D_FF % MXU_TILE == 0
GLA_TILE = 4096
GLA_HALF = 256
ATT_DILATIONS = tuple(sorted(d for _, d in DILATED_PATTERNS))
ATT_SUPER = max(ATT_DILATIONS) * ATT_BLOCK
ATT_COMBINE_ROWS = 256
NEG = -1e30
LOG2E = math.log2(math.e)

assert all(w // d == ATT_BLOCK for w, d in DILATED_PATTERNS)
assert ATT_DILATIONS[0] == 1 and all(ATT_SUPER % (d * ATT_BLOCK) == 0 for d in ATT_DILATIONS)
assert ATT_DILATIONS == (1, 4, 16)


def _const_spec(shape):
    nd = len(shape)
    return pl.BlockSpec(shape, lambda *_: (0,) * nd, pipeline_mode=pl.Buffered(1))


def _rms(x, g):
    return x * lax.rsqrt(jnp.mean(x * x, axis=-1, keepdims=True) + EPS) * g


def _swiglu_half_step(h, w1_ref, w3_ref, w2_ref):
    acc = None
    for lo, hi in zip(FF_SPLITS[:-1], FF_SPLITS[1:]):
        sl = slice(lo, hi)
        a = jnp.dot(h, w1_ref[:, sl], preferred_element_type=F32)
        b = jnp.dot(h, w3_ref[:, sl], preferred_element_type=F32)
        g = (a / (1.0 + jnp.exp(-a)) * b).astype(BF16)
        part = jnp.dot(g, w2_ref[sl, :], preferred_element_type=F32)
        acc = part if acc is None else acc + part
    return acc


def _ffn_proj_kernel(x_ref, pos_ref, g1_ref, w1_ref, w3_ref, w2_ref, gmix_ref, wgla_ref, wga_ref,
                     wa2_ref, ba_ref, watt_ref, freq_ref, sgn1_ref, sgn2_ref,
                     x1_ref, gq_ref, gk_ref, gv_ref, gr_ref, la_ref, aq_ref, ak_ref, av_ref):
    groups = [slice(r, r + ROW_GROUP) for r in range(0, TOKEN_TILE, ROW_GROUP)]
    x1s = []
    for rows in groups:
        x = x_ref[rows, :]
        h = _rms(x, g1_ref[...]).astype(BF16)
        x1 = x + 0.5 * _swiglu_half_step(h, w1_ref, w3_ref, w2_ref)
        x1_ref[rows, :] = x1
        x1s.append(x1)

    pos_col = jnp.broadcast_to(pos_ref[...], (SUBLANES, TOKEN_TILE)).T[:, 0:1]
    for rows, x1 in zip(groups, x1s):
        h2 = _rms(x1, gmix_ref[...]).astype(BF16)
        pa = jnp.dot(h2, watt_ref[...], preferred_element_type=F32)
        av_ref[rows, :] = pa[:, 2 * ATT_WIDTH:].astype(BF16)

        ang = pos_col[rows, :] * freq_ref[...]
        cos = jnp.cos(ang)
        sin = jnp.sin(ang)
        s1 = sin * sgn1_ref[...]
        s2 = sin * sgn2_ref[...]
        half = ROT_DIM // 2
        for off, ref, scale in ((0, aq_ref, ATT_HD ** -0.5 * LOG2E), (ATT_WIDTH, ak_ref, 1.0)):
            for cg in range(ATT_WIDTH // LANES):
                t = pa[:, off + cg * LANES: off + (cg + 1) * LANES]
                r = t * cos + pltpu.roll(t, LANES - half, 1) * s1 + pltpu.roll(t, half, 1) * s2
                ref[rows, cg * LANES:(cg + 1) * LANES] = (r * scale).astype(BF16)

        ga = jnp.dot(h2, wga_ref[...], preferred_element_type=F32).astype(BF16)
        z = jnp.dot(ga, wa2_ref[...], preferred_element_type=F32) + ba_ref[...]
        la_ref[rows, :] = (jnp.minimum(z, 0.0) - jnp.log(1.0 + jnp.exp(-jnp.abs(z)))) * (1.0 / GLA_TAU)

        pg = jnp.dot(h2, wgla_ref[...], preferred_element_type=F32)
        gq_ref[rows, :] = (pg[:, :GLA_KW] * (GLA_DK ** -0.5)).astype(BF16)
        gk_ref[rows, :] = pg[:, GLA_KW:2 * GLA_KW].astype(BF16)
        gv_ref[rows, :] = pg[:, 2 * GLA_KW:2 * GLA_KW + GLA_WIDTH].astype(BF16)
        gr = pg[:, 2 * GLA_KW + GLA_WIDTH:]
        gr_ref[rows, :] = (gr / (1.0 + jnp.exp(-gr))).astype(BF16)


def _ffn_proj(x2d, pos, g1, w1, w3, w2, gmix, wgla, wga, wa2, ba, watt, freq, sgn1, sgn2):
    T = x2d.shape[0]
    tm = TOKEN_TILE
    row = lambda n: pl.BlockSpec((tm, n), lambda i: (i, 0))
    pos_spec = pl.BlockSpec((None, 1, tm), lambda i: (i, 0, 0))
    in_specs = [row(D_MODEL), pos_spec, _const_spec(g1.shape), _const_spec(w1.shape), _const_spec(w3.shape),
                _const_spec(w2.shape), _const_spec(gmix.shape), _const_spec(wgla.shape),
                _const_spec(wga.shape), _const_spec(wa2.shape), _const_spec(ba.shape),
                _const_spec(watt.shape), _const_spec(freq.shape), _const_spec(sgn1.shape),
                _const_spec(sgn2.shape)]
    out_shape = [jax.ShapeDtypeStruct((T, D_MODEL), F32),
                 jax.ShapeDtypeStruct((T, GLA_KW), BF16), jax.ShapeDtypeStruct((T, GLA_KW), BF16),
                 jax.ShapeDtypeStruct((T, GLA_WIDTH), BF16), jax.ShapeDtypeStruct((T, GLA_WIDTH), BF16),
                 jax.ShapeDtypeStruct((T, GLA_KW), F32),
                 jax.ShapeDtypeStruct((T, ATT_WIDTH), BF16), jax.ShapeDtypeStruct((T, ATT_WIDTH), BF16),
                 jax.ShapeDtypeStruct((T, ATT_WIDTH), BF16)]
    return pl.pallas_call(
        _ffn_proj_kernel, grid=(T // tm,), in_specs=in_specs,
        out_specs=[row(s.shape[1]) for s in out_shape], out_shape=out_shape,
        compiler_params=pltpu.CompilerParams(dimension_semantics=("arbitrary",),
                                             vmem_limit_bytes=FFN_PROJ_VMEM_LIMIT_BYTES),
        name="ffn_proj")(x2d, pos, g1, w1, w3, w2, gmix, wgla, wga, wa2, ba, watt, freq, sgn1, sgn2)


def _gla_kernel(q_ref, k_ref, v_ref, la_ref, r_ref, gn_ref, ltri_ref, o_ref, s_ref):
    @pl.when(pl.program_id(1) == 0)
    def _():
        s_ref[...] = jnp.zeros_like(s_ref)

    C = GLA_CHUNK
    r_kk = lax.broadcasted_iota(jnp.int32, (GLA_KW, GLA_KW), 0) // C
    c_kk = lax.broadcasted_iota(jnp.int32, (GLA_KW, GLA_KW), 1) // C
    bd_k = r_kk == c_kk
    r_kv = lax.broadcasted_iota(jnp.int32, (GLA_KW, GLA_WIDTH), 0) // C
    c_kv = lax.broadcasted_iota(jnp.int32, (GLA_KW, GLA_WIDTH), 1) // GLA_DV
    bd_v = r_kv == c_kv
    causal = (lax.broadcasted_iota(jnp.int32, (C, GLA_KW), 1) % C
              <= lax.broadcasted_iota(jnp.int32, (C, GLA_KW), 0))
    ltri = ltri_ref[...]

    for hf in range(GLA_TILE // GLA_HALF):
        base = hf * GLA_HALF
        g = la_ref[base:base + GLA_HALF, :]
        g_hi = g.astype(BF16)
        g_lo = (g - g_hi.astype(F32)).astype(BF16)
        b = (jnp.dot(ltri, g_hi, preferred_element_type=F32)
             + jnp.dot(ltri, g_lo, preferred_element_type=F32))
        for c in range(GLA_HALF // C):
            lo = base + c * C
            bc = b[c * C:(c + 1) * C, :]
            bl = bc[C - 1:C, :]
            qc = q_ref[lo:lo + C, :].astype(F32)
            kc = k_ref[lo:lo + C, :].astype(F32)
            vc = v_ref[lo:lo + C, :]
            qd = (qc * jnp.exp(bc)).astype(BF16)
            k_inv = kc * jnp.exp(-bc)
            ki = k_inv.astype(BF16)
            kt = k_inv * jnp.exp(bl)
            kbd = jnp.where(bd_k, jnp.tile(ki, (GLA_HEADS, 1)), jnp.zeros((), BF16))
            a = lax.dot_general(qd, kbd, (((1,), (1,)), ((), ())), preferred_element_type=F32)
            a = jnp.where(causal, a, 0.0).astype(BF16)
            vbd = jnp.where(bd_v, jnp.tile(vc, (GLA_HEADS, 1)), jnp.zeros((), BF16))
            s_prev = s_ref[...]
            sbd = jnp.where(bd_v, jnp.tile(s_prev.astype(BF16), (1, GLA_HEADS)), jnp.zeros((), BF16))
            o = (jnp.dot(a, vbd, preferred_element_type=F32)
                 + jnp.dot(qd, sbd, preferred_element_type=F32))

            kt_t = kt.T.astype(BF16)
            dec = jnp.exp(bc.T[:, C - 1:C])
            u = jnp.concatenate(
                [jnp.dot(kt_t[h * GLA_DK:(h + 1) * GLA_DK, :], vc[:, h * GLA_DV:(h + 1) * GLA_DV],
                         preferred_element_type=F32) for h in range(GLA_HEADS)], axis=0)
            s_ref[...] = dec * s_prev + u

            gate = r_ref[lo:lo + C, :].astype(F32)
            for h in range(GLA_HEADS):
                hs = slice(h * GLA_DV, (h + 1) * GLA_DV)
                oh = _rms(o[:, hs], gn_ref[:, hs])
                o_ref[lo:lo + C, hs] = (oh * gate[:, hs]).astype(BF16)


def _gla(gq, gk, gv, la, gr, gn, ltri, batch):
    T = gq.shape[0]
    tiles = T // batch // GLA_TILE
    row = lambda n: pl.BlockSpec((GLA_TILE, n), lambda b, t: (b * tiles + t, 0))
    return pl.pallas_call(
        _gla_kernel, grid=(batch, tiles),
        in_specs=[row(GLA_KW), row(GLA_KW), row(GLA_WIDTH), row(GLA_KW), row(GLA_WIDTH),
                  _const_spec(gn.shape), _const_spec(ltri.shape)],
        out_specs=row(GLA_WIDTH),
        out_shape=jax.ShapeDtypeStruct((T, GLA_WIDTH), BF16),
        scratch_shapes=[pltpu.VMEM((GLA_KW, GLA_DV), F32)],
        compiler_params=pltpu.CompilerParams(dimension_semantics=("arbitrary", "arbitrary"),
                                             vmem_limit_bytes=VMEM_LIMIT_BYTES),
        name="gla")(gq, gk, gv, la, gr, gn, ltri)


def _attn_kernel(n_late, q_ref, k_ref, v_ref, *rest):
    late_in, o_ref, late_out = rest[:n_late], rest[n_late], rest[n_late + 1:2 * n_late + 1]
    xf_ref, x4_ref, *scratch = rest[2 * n_late + 1:]
    for src, dst in zip(late_in, late_out):
        dst[...] = src[...].astype(BF16)

    B = ATT_BLOCK
    T = ATT_SUPER
    nd = len(ATT_DILATIONS)
    qd = dict(zip(ATT_DILATIONS[1:], scratch[:nd - 1]))
    kc = dict(zip(ATT_DILATIONS, scratch[nd - 1:2 * nd - 1]))
    vc = dict(zip(ATT_DILATIONS, scratch[2 * nd - 1:3 * nd - 1]))
    acc = scratch[3 * nd - 1:]
    acc_o = dict(zip(ATT_DILATIONS, acc[0::2]))
    acc_l = dict(zip(ATT_DILATIONS, acc[1::2]))
    tile = pl.program_id(2)

    def stage(src_ref, dst, halo):
        xf_ref[...] = src_ref[...].astype(F32)
        n4 = T // 4
        for r in range(4):
            c4 = xf_ref[pl.ds(r, n4, stride=4), :]
            x4_ref[r * n4:(r + 1) * n4, :] = c4
            lo_row = r * (n4 + halo) + halo
            dst[4][lo_row:lo_row + n4, :] = c4.astype(BF16)
        n16 = T // 16
        for r in range(16):
            lo_row = r * (n16 + halo) + halo
            dst[16][lo_row:lo_row + n16, :] = (
                x4_ref[pl.ds((r % 4) * n4 + r // 4, n16, stride=4), :].astype(BF16))

    @pl.when(tile == 0)
    def _():
        zeros = jnp.zeros((B, LANES), BF16)
        for d in ATT_DILATIONS:
            for r in range(d):
                row = r * (T // d + B)
                kc[d][row:row + B, :] = zeros
                vc[d][row:row + B, :] = zeros

    stage(q_ref, qd, 0)
    stage(k_ref, kc, B)
    stage(v_ref, vc, B)
    kc[1][B:, :] = k_ref[...]
    vc[1][B:, :] = v_ref[...]

    qi = lax.broadcasted_iota(jnp.int32, (B, 2 * B), 0)
    ki = lax.broadcasted_iota(jnp.int32, (B, 2 * B), 1)
    band = (ki >= qi) & (ki <= qi + B)
    bias_band = jnp.where(band, 0.0, NEG).astype(F32)
    bias_first = jnp.where(band & ((ki >= B) | (tile > 0)), 0.0, NEG).astype(F32)
    lane = lax.broadcasted_iota(jnp.int32, (B, LANES), 1)
    lo = lane < ATT_HD
    zero_bf = jnp.zeros((), BF16)
    ones_v = jnp.ones((2 * B, LANES), BF16)

    def block(q_src, q_row, d, k_row, bias, rows_out):
        qp = q_src[q_row:q_row + B, :]
        kk = kc[d][k_row:k_row + 2 * B, :]
        vv = vc[d][k_row:k_row + 2 * B, :]
        qs = jnp.concatenate([jnp.where(lo, qp, zero_bf), jnp.where(lo, zero_bf, qp)], axis=0)
        s = lax.dot_general(qs, kk, (((1,), (1,)), ((), ())), preferred_element_type=F32)
        s = s + jnp.concatenate([bias, bias], axis=0)
        m = jnp.max(s, axis=-1, keepdims=True)
        p = jnp.exp2(s - m).astype(BF16)
        pv = jnp.dot(p, jnp.concatenate([vv, ones_v], axis=1), preferred_element_type=F32)
        den = jnp.where(lo, pv[:B, LANES:], pv[B:, LANES:])
        acc_o[d][rows_out, :] = jnp.where(lo, pv[:B, :LANES], pv[B:, :LANES]) / den
        acc_l[d][rows_out, :] = jnp.where(lo, m[:B], m[B:]) + jnp.log2(den)

    for d in ATT_DILATIONS:
        n = T // d
        for r in range(d):
            for jb in range(n // B):
                q_src, q_row = (q_ref, jb * B) if d == 1 else (qd[d], r * n + jb * B)
                rows_out = pl.ds(r + d * B * jb, B, stride=d) if d > 1 else pl.ds(jb * B, B)
                block(q_src, q_row, d, r * (n + B) + jb * B, bias_first if jb == 0 else bias_band,
                      rows_out)

    for c in range(T // ATT_COMBINE_ROWS):
        rows = slice(c * ATT_COMBINE_ROWS, (c + 1) * ATT_COMBINE_ROWS)
        ls = [acc_l[d][rows, :] for d in ATT_DILATIONS]
        mx = functools.reduce(jnp.maximum, ls)
        es = [jnp.exp2(l - mx) for l in ls]
        num = functools.reduce(jnp.add, [e * acc_o[d][rows, :] for e, d in zip(es, ATT_DILATIONS)])
        o_ref[rows, :] = (num / functools.reduce(jnp.add, es)).astype(o_ref.dtype)

    for d in ATT_DILATIONS:
        n = T // d
        for r in range(d):
            row = r * (n + B)
            kc[d][row:row + B, :] = kc[d][row + n:row + n + B, :]
            vc[d][row:row + B, :] = vc[d][row + n:row + n + B, :]


def _late_chunk_spec(shape, steps, step_index):
    rows, cols = shape
    share = 1 if (rows // steps) % BF16_SUBLANES == 0 else 2
    assert rows % (steps // share) == 0 and (rows * share // steps) % BF16_SUBLANES == 0
    return pl.BlockSpec((rows * share // steps, cols), lambda *idx: (step_index(*idx) // share, 0))


def _attn(q, k, v, late, batch, seq):
    T = q.shape[0]
    tiles = seq // ATT_SUPER
    pairs = ATT_WIDTH // LANES
    spec = pl.BlockSpec((ATT_SUPER, LANES), lambda b, h, t: (b * tiles + t, h))
    late_specs = [_late_chunk_spec(w.shape, batch * pairs * tiles,
                                   lambda b, h, t: (b * pairs + h) * tiles + t) for w in late]
    rows = lambda n, dt: pltpu.VMEM((n, LANES), dt)
    kv_rows = [ATT_SUPER + d * ATT_BLOCK for d in ATT_DILATIONS]
    scratch = ([rows(ATT_SUPER, F32)] * 2 + [rows(ATT_SUPER, BF16) for _ in ATT_DILATIONS[1:]]
               + [rows(n, BF16) for n in kv_rows] + [rows(n, BF16) for n in kv_rows]
               + [rows(ATT_SUPER, F32) for _ in range(2 * len(ATT_DILATIONS))])
    res = pl.pallas_call(
        functools.partial(_attn_kernel, len(late)), grid=(batch, pairs, tiles),
        in_specs=[spec, spec, spec] + late_specs, out_specs=[spec] + late_specs,
        out_shape=[jax.ShapeDtypeStruct((T, ATT_WIDTH), BF16)]
        + [jax.ShapeDtypeStruct(w.shape, BF16) for w in late],
        scratch_shapes=scratch,
        compiler_params=pltpu.CompilerParams(
            dimension_semantics=("arbitrary", "arbitrary", "arbitrary"),
            vmem_limit_bytes=VMEM_LIMIT_BYTES),
        name="attn")(q, k, v, *late)
    return res[0], res[1:]


def _out_ffn_kernel(x1_ref, og_ref, oa_ref, ga_ref, wo_ref, g2_ref, w1_ref, w3_ref, w2_ref, gf_ref,
                    out_ref):
    groups = [slice(r, r + ROW_GROUP) for r in range(0, OUT_TOKEN_TILE, ROW_GROUP)]
    x2s = []
    for rows in groups:
        oa = _rms(oa_ref[rows, :].astype(F32), ga_ref[...]).astype(BF16)
        x2s.append(x1_ref[rows, :]
                   + jnp.dot(og_ref[rows, :], wo_ref[:GLA_WIDTH, :], preferred_element_type=F32)
                   + jnp.dot(oa, wo_ref[GLA_WIDTH:, :], preferred_element_type=F32))
    for rows, x2 in zip(groups, x2s):
        h = _rms(x2, g2_ref[...]).astype(BF16)
        x3 = x2 + 0.5 * _swiglu_half_step(h, w1_ref, w3_ref, w2_ref)
        out_ref[rows, :] = _rms(x3, gf_ref[...])


def _out_ffn(x1, og, oa, ga, wo, g2, w1, w3, w2, gf):
    T = x1.shape[0]
    tm = OUT_TOKEN_TILE
    row = lambda n: pl.BlockSpec((tm, n), lambda i: (i, 0))
    return pl.pallas_call(
        _out_ffn_kernel, grid=(T // tm,),
        in_specs=[row(D_MODEL), row(GLA_WIDTH), row(ATT_WIDTH), _const_spec(ga.shape),
                  _const_spec(wo.shape), _const_spec(g2.shape),
                  _const_spec(w1.shape), _const_spec(w3.shape), _const_spec(w2.shape),
                  _const_spec(gf.shape)],
        out_specs=row(D_MODEL),
        out_shape=jax.ShapeDtypeStruct((T, D_MODEL), F32),
        compiler_params=pltpu.CompilerParams(dimension_semantics=("arbitrary",),
                                             vmem_limit_bytes=VMEM_LIMIT_BYTES),
        name="out_ffn")(x1, og, oa, ga, wo, g2, w1, w3, w2, gf)


def _rope_tables():
    j = np.arange(LANES) % ATT_HD
    half = ROT_DIM // 2
    inv_freq = ROPE_THETA ** (-np.arange(0, ROT_DIM, 2, dtype=np.float32) / ROT_DIM)
    freq = np.where(j < ROT_DIM, inv_freq[j % half], 0.0).astype(np.float32)
    sgn1 = np.where(j < half, -1.0, 0.0).astype(np.float32)
    sgn2 = np.where((j >= half) & (j < ROT_DIM), 1.0, 0.0).astype(np.float32)
    return freq[None, :], sgn1[None, :], sgn2[None, :]


def _chunk_tril():
    i = np.arange(GLA_HALF)
    same = (i[:, None] // GLA_CHUNK) == (i[None, :] // GLA_CHUNK)
    return jnp.asarray((same & (i[None, :] <= i[:, None])).astype(np.float32), dtype=BF16)


def kernel(x, positions, ffn1_norm, ffn1_w1, ffn1_w3, ffn1_w2, mix_norm, w_in, gla_w_a2, gla_b_a,
           gla_out_norm, att_out_norm, w_out, ffn2_norm, ffn2_w1, ffn2_w3, ffn2_w2, final_norm):
    batch, seq, _ = x.shape
    T = batch * seq
    depth = ffn1_norm.shape[0]
    assert depth == 1, "the final norm is fused into the single layer's last kernel"
    freq, sgn1, sgn2 = (jnp.asarray(t) for t in _rope_tables())
    ltri = _chunk_tril()
    pos = positions.astype(F32).reshape(T // TOKEN_TILE, 1, TOKEN_TILE)
    xs = x.reshape(T, D_MODEL)

    o_gla_end = 2 * GLA_KW + 2 * GLA_WIDTH
    o_ga_end = o_gla_end + GLA_RANK
    for l in range(depth):
        wi = w_in[l]
        wgla = wi[:, :o_gla_end].astype(BF16)
        wga = jnp.pad(wi[:, o_gla_end:o_ga_end].astype(BF16), ((0, 0), (0, LANES - GLA_RANK)))
        watt = wi[:, o_ga_end:].astype(BF16)
        wa2 = jnp.pad(gla_w_a2[l].astype(BF16), ((0, LANES - GLA_RANK), (0, 0)))
        x1, gq, gk, gv, gr, la, aq, ak, av = _ffn_proj(
            xs, pos, ffn1_norm[l][None, :], ffn1_w1[l].astype(BF16), ffn1_w3[l].astype(BF16),
            ffn1_w2[l].astype(BF16), mix_norm[l][None, :], wgla, wga, wa2, gla_b_a[l][None, :], watt,
            freq, sgn1, sgn2)

        o_gla = _gla(gq, gk, gv, la, gr, gla_out_norm[l][None, :], ltri, batch)

        o_att, (w1b, w3b, w2b, wob) = _attn(
            aq, ak, av, (ffn2_w1[l], ffn2_w3[l], ffn2_w2[l], w_out[l]), batch, seq)

        xs = _out_ffn(x1, o_gla, o_att, att_out_norm[l][None, :], wob, ffn2_norm[l][None, :], w1b, w3b,
                      w2b, final_norm[None, :])
    return xs.reshape(batch, seq, D_MODEL)
```

```python
import functools
import math

import jax
import jax.numpy as jnp
import numpy as np
from jax import lax
from jax.experimental import pallas as pl
from jax.experimental.pallas import tpu as pltpu

F32 = jnp.float32
BF16 = jnp.bfloat16

D_MODEL = 1024
D_FF = 2816
GLA_WIDTH = 512
GLA_HEADS = 4
GLA_DV = 128
GLA_DK = 64
GLA_KW = GLA_HEADS * GLA_DK
GLA_RANK = 16
GLA_TAU = 16.0
GLA_CHUNK = 64
ATT_WIDTH = 512
ATT_HEADS = 8
ATT_HD = 64
ROT_DIM = 16
ROPE_THETA = 500000.0
DILATED_PATTERNS = ((128, 1), (512, 4), (2048, 16))
ATT_BLOCK = 128
EPS = 1e-6

LANES = 128
SUBLANES = 8
BF16_SUBLANES = 16
VMEM_LIMIT_BYTES = 56 * 1024 * 1024
FFN_PROJ_VMEM_LIMIT_BYTES = 60 * 1024 * 1024

TOKEN_TILE = 1024
OUT_TOKEN_TILE = 1024
ROW_GROUP = 256
MXU_TILE = 256
FF_SPLITS = (0, (D_FF // MXU_TILE // 2) * MXU_TILE, D_FF)
assert D_FF % MXU_TILE == 0
GLA_TILE = 2048
GLA_HALF = 256
ATT_DILATIONS = tuple(sorted(d for _, d in DILATED_PATTERNS))
ATT_SUPER = max(ATT_DILATIONS) * ATT_BLOCK
ATT_COMBINE_ROWS = 256
NEG = -1e30
LOG2E = math.log2(math.e)

assert all(w // d == ATT_BLOCK for w, d in DILATED_PATTERNS)
assert ATT_DILATIONS[0] == 1 and all(ATT_SUPER % (d * ATT_BLOCK) == 0 for d in ATT_DILATIONS)
assert ATT_DILATIONS == (1, 4, 16)


def _const_spec(shape):
    nd = len(shape)
    return pl.BlockSpec(shape, lambda *_: (0,) * nd, pipeline_mode=pl.Buffered(1))


def _rms(x, g):
    return x * lax.rsqrt(jnp.mean(x * x, axis=-1, keepdims=True) + EPS) * g


def _swiglu_half_step(h, w1_ref, w3_ref, w2_ref):
    acc = None
    for lo, hi in zip(FF_SPLITS[:-1], FF_SPLITS[1:]):
        sl = slice(lo, hi)
        a = jnp.dot(h, w1_ref[:, sl], preferred_element_type=F32)
        b = jnp.dot(h, w3_ref[:, sl], preferred_element_type=F32)
        g = (a / (1.0 + jnp.exp(-a)) * b).astype(BF16)
        part = jnp.dot(g, w2_ref[sl, :], preferred_element_type=F32)
        acc = part if acc is None else acc + part
    return acc


def _ffn_proj_kernel(x_ref, pos_ref, g1_ref, w1_ref, w3_ref, w2_ref, gmix_ref, wgla_ref, wga_ref,
                     wa2_ref, ba_ref, watt_ref, freq_ref, sgn1_ref, sgn2_ref,
                     x1_ref, gq_ref, gk_ref, gv_ref, gr_ref, la_ref, aq_ref, ak_ref, av_ref):
    groups = [slice(r, r + ROW_GROUP) for r in range(0, TOKEN_TILE, ROW_GROUP)]
    x1s = []
    for rows in groups:
        x = x_ref[rows, :]
        h = _rms(x, g1_ref[...]).astype(BF16)
        x1 = x + 0.5 * _swiglu_half_step(h, w1_ref, w3_ref, w2_ref)
        x1_ref[rows, :] = x1
        x1s.append(x1)

    pos_col = jnp.broadcast_to(pos_ref[...], (SUBLANES, TOKEN_TILE)).T[:, 0:1]
    for rows, x1 in zip(groups, x1s):
        h2 = _rms(x1, gmix_ref[...]).astype(BF16)
        pa = jnp.dot(h2, watt_ref[...], preferred_element_type=F32)
        av_ref[rows, :] = pa[:, 2 * ATT_WIDTH:].astype(BF16)

        ang = pos_col[rows, :] * freq_ref[...]
        cos = jnp.cos(ang)
        sin = jnp.sin(ang)
        s1 = sin * sgn1_ref[...]
        s2 = sin * sgn2_ref[...]
        half = ROT_DIM // 2
        for off, ref, scale in ((0, aq_ref, ATT_HD ** -0.5 * LOG2E), (ATT_WIDTH, ak_ref, 1.0)):
            for cg in range(ATT_WIDTH // LANES):
                t = pa[:, off + cg * LANES: off + (cg + 1) * LANES]
                r = t * cos + pltpu.roll(t, LANES - half, 1) * s1 + pltpu.roll(t, half, 1) * s2
                ref[rows, cg * LANES:(cg + 1) * LANES] = (r * scale).astype(BF16)

        ga = jnp.dot(h2, wga_ref[...], preferred_element_type=F32).astype(BF16)
        z = jnp.dot(ga, wa2_ref[...], preferred_element_type=F32) + ba_ref[...]
        la_ref[rows, :] = (jnp.minimum(z, 0.0) - jnp.log(1.0 + jnp.exp(-jnp.abs(z)))) * (1.0 / GLA_TAU)

        pg = jnp.dot(h2, wgla_ref[...], preferred_element_type=F32)
        gq_ref[rows, :] = (pg[:, :GLA_KW] * (GLA_DK ** -0.5)).astype(BF16)
        gk_ref[rows, :] = pg[:, GLA_KW:2 * GLA_KW].astype(BF16)
        gv_ref[rows, :] = pg[:, 2 * GLA_KW:2 * GLA_KW + GLA_WIDTH].astype(BF16)
        gr = pg[:, 2 * GLA_KW + GLA_WIDTH:]
        gr_ref[rows, :] = (gr / (1.0 + jnp.exp(-gr))).astype(BF16)


def _ffn_proj(x2d, pos, g1, w1, w3, w2, gmix, wgla, wga, wa2, ba, watt, freq, sgn1, sgn2):
    T = x2d.shape[0]
    tm = TOKEN_TILE
    row = lambda n: pl.BlockSpec((tm, n), lambda i: (i, 0))
    pos_spec = pl.BlockSpec((None, 1, tm), lambda i: (i, 0, 0))
    in_specs = [row(D_MODEL), pos_spec, _const_spec(g1.shape), _const_spec(w1.shape), _const_spec(w3.shape),
                _const_spec(w2.shape), _const_spec(gmix.shape), _const_spec(wgla.shape),
                _const_spec(wga.shape), _const_spec(wa2.shape), _const_spec(ba.shape),
                _const_spec(watt.shape), _const_spec(freq.shape), _const_spec(sgn1.shape),
                _const_spec(sgn2.shape)]
    out_shape = [jax.ShapeDtypeStruct((T, D_MODEL), F32),
                 jax.ShapeDtypeStruct((T, GLA_KW), BF16), jax.ShapeDtypeStruct((T, GLA_KW), BF16),
                 jax.ShapeDtypeStruct((T, GLA_WIDTH), BF16), jax.ShapeDtypeStruct((T, GLA_WIDTH), BF16),
                 jax.ShapeDtypeStruct((T, GLA_KW), F32),
                 jax.ShapeDtypeStruct((T, ATT_WIDTH), BF16), jax.ShapeDtypeStruct((T, ATT_WIDTH), BF16),
                 jax.ShapeDtypeStruct((T, ATT_WIDTH), BF16)]
    return pl.pallas_call(
        _ffn_proj_kernel, grid=(T // tm,), in_specs=in_specs,
        out_specs=[row(s.shape[1]) for s in out_shape], out_shape=out_shape,
        compiler_params=pltpu.CompilerParams(dimension_semantics=("arbitrary",),
                                             vmem_limit_bytes=FFN_PROJ_VMEM_LIMIT_BYTES),
        name="ffn_proj")(x2d, pos, g1, w1, w3, w2, gmix, wgla, wga, wa2, ba, watt, freq, sgn1, sgn2)


def _gla_kernel(q_ref, k_ref, v_ref, la_ref, r_ref, gn_ref, ltri_ref, o_ref, s_ref):
    @pl.when(pl.program_id(1) == 0)
    def _():
        s_ref[...] = jnp.zeros_like(s_ref)

    C = GLA_CHUNK
    r_kk = lax.broadcasted_iota(jnp.int32, (GLA_KW, GLA_KW), 0) // C
    c_kk = lax.broadcasted_iota(jnp.int32, (GLA_KW, GLA_KW), 1) // C
    bd_k = r_kk == c_kk
    r_kv = lax.broadcasted_iota(jnp.int32, (GLA_KW, GLA_WIDTH), 0) // C
    c_kv = lax.broadcasted_iota(jnp.int32, (GLA_KW, GLA_WIDTH), 1) // GLA_DV
    bd_v = r_kv == c_kv
    causal = (lax.broadcasted_iota(jnp.int32, (C, GLA_KW), 1) % C
              <= lax.broadcasted_iota(jnp.int32, (C, GLA_KW), 0))
    ltri = ltri_ref[...]

    for hf in range(GLA_TILE // GLA_HALF):
        base = hf * GLA_HALF
        g = la_ref[base:base + GLA_HALF, :]
        g_hi = g.astype(BF16)
        g_lo = (g - g_hi.astype(F32)).astype(BF16)
        b = (jnp.dot(ltri, g_hi, preferred_element_type=F32)
             + jnp.dot(ltri, g_lo, preferred_element_type=F32))
        for c in range(GLA_HALF // C):
            lo = base + c * C
            bc = b[c * C:(c + 1) * C, :]
            bl = bc[C - 1:C, :]
            qc = q_ref[lo:lo + C, :].astype(F32)
            kc = k_ref[lo:lo + C, :].astype(F32)
            vc = v_ref[lo:lo + C, :]
            qd = (qc * jnp.exp(bc)).astype(BF16)
            k_inv = kc * jnp.exp(-bc)
            ki = k_inv.astype(BF16)
            kt = k_inv * jnp.exp(bl)
            kbd = jnp.where(bd_k, jnp.tile(ki, (GLA_HEADS, 1)), jnp.zeros((), BF16))
            a = lax.dot_general(qd, kbd, (((1,), (1,)), ((), ())), preferred_element_type=F32)
            a = jnp.where(causal, a, 0.0).astype(BF16)
            vbd = jnp.where(bd_v, jnp.tile(vc, (GLA_HEADS, 1)), jnp.zeros((), BF16))
            s_prev = s_ref[...]
            sbd = jnp.where(bd_v, jnp.tile(s_prev.astype(BF16), (1, GLA_HEADS)), jnp.zeros((), BF16))
            o = (jnp.dot(a, vbd, preferred_element_type=F32)
                 + jnp.dot(qd, sbd, preferred_element_type=F32))

            kt_t = kt.T.astype(BF16)
            dec = jnp.exp(bc.T[:, C - 1:C])
            u = jnp.concatenate(
                [jnp.dot(kt_t[h * GLA_DK:(h + 1) * GLA_DK, :], vc[:, h * GLA_DV:(h + 1) * GLA_DV],
                         preferred_element_type=F32) for h in range(GLA_HEADS)], axis=0)
            s_ref[...] = dec * s_prev + u

            gate = r_ref[lo:lo + C, :].astype(F32)
            for h in range(GLA_HEADS):
                hs = slice(h * GLA_DV, (h + 1) * GLA_DV)
                oh = _rms(o[:, hs], gn_ref[:, hs])
                o_ref[lo:lo + C, hs] = (oh * gate[:, hs]).astype(BF16)


def _gla(gq, gk, gv, la, gr, gn, ltri, batch):
    T = gq.shape[0]
    tiles = T // batch // GLA_TILE
    row = lambda n: pl.BlockSpec((GLA_TILE, n), lambda b, t: (b * tiles + t, 0))
    return pl.pallas_call(
        _gla_kernel, grid=(batch, tiles),
        in_specs=[row(GLA_KW), row(GLA_KW), row(GLA_WIDTH), row(GLA_KW), row(GLA_WIDTH),
                  _const_spec(gn.shape), _const_spec(ltri.shape)],
        out_specs=row(GLA_WIDTH),
        out_shape=jax.ShapeDtypeStruct((T, GLA_WIDTH), BF16),
        scratch_shapes=[pltpu.VMEM((GLA_KW, GLA_DV), F32)],
        compiler_params=pltpu.CompilerParams(dimension_semantics=("arbitrary", "arbitrary"),
                                             vmem_limit_bytes=VMEM_LIMIT_BYTES),
        name="gla")(gq, gk, gv, la, gr, gn, ltri)


def _attn_kernel(n_late, q_ref, k_ref, v_ref, *rest):
    late_in, o_ref, late_out = rest[:n_late], rest[n_late], rest[n_late + 1:2 * n_late + 1]
    xf_ref, x4_ref, *scratch = rest[2 * n_late + 1:]
    for src, dst in zip(late_in, late_out):
        dst[...] = src[...].astype(BF16)

    B = ATT_BLOCK
    T = ATT_SUPER
    nd = len(ATT_DILATIONS)
    qd = dict(zip(ATT_DILATIONS[1:], scratch[:nd - 1]))
    kc = dict(zip(ATT_DILATIONS, scratch[nd - 1:2 * nd - 1]))
    vc = dict(zip(ATT_DILATIONS, scratch[2 * nd - 1:3 * nd - 1]))
    acc = scratch[3 * nd - 1:]
    acc_o = dict(zip(ATT_DILATIONS, acc[0::2]))
    acc_l = dict(zip(ATT_DILATIONS, acc[1::2]))
    tile = pl.program_id(2)

    def stage(src_ref, dst, halo):
        xf_ref[...] = src_ref[...].astype(F32)
        n4 = T // 4
        for r in range(4):
            c4 = xf_ref[pl.ds(r, n4, stride=4), :]
            x4_ref[r * n4:(r + 1) * n4, :] = c4
            lo_row = r * (n4 + halo) + halo
            dst[4][lo_row:lo_row + n4, :] = c4.astype(BF16)
        n16 = T // 16
        for r in range(16):
            lo_row = r * (n16 + halo) + halo
            dst[16][lo_row:lo_row + n16, :] = (
                x4_ref[pl.ds((r % 4) * n4 + r // 4, n16, stride=4), :].astype(BF16))

    @pl.when(tile == 0)
    def _():
        zeros = jnp.zeros((B, LANES), BF16)
        for d in ATT_DILATIONS:
            for r in range(d):
                row = r * (T // d + B)
                kc[d][row:row + B, :] = zeros
                vc[d][row:row + B, :] = zeros

    stage(q_ref, qd, 0)
    stage(k_ref, kc, B)
    stage(v_ref, vc, B)
    kc[1][B:, :] = k_ref[0:B, :]
    vc[1][B:, :] = v_ref[0:B, :]

    qi = lax.broadcasted_iota(jnp.int32, (B, 2 * B), 0)
    ki = lax.broadcasted_iota(jnp.int32, (B, 2 * B), 1)
    band = (ki >= qi) & (ki <= qi + B)
    bias_band = jnp.where(band, 0.0, NEG).astype(F32)
    bias_first = jnp.where(band & ((ki >= B) | (tile > 0)), 0.0, NEG).astype(F32)
    lane = lax.broadcasted_iota(jnp.int32, (B, LANES), 1)
    lo = lane < ATT_HD
    zero_bf = jnp.zeros((), BF16)
    ones_v = jnp.ones((2 * B, LANES), BF16)

    def block(q_src, q_row, d, k_row, bias, rows_out):
        qp = q_src[q_row:q_row + B, :]
        if d == 1 and k_row > 0:
            kk = k_ref[k_row - B:k_row + B, :]
            vv = v_ref[k_row - B:k_row + B, :]
        else:
            kk = kc[d][k_row:k_row + 2 * B, :]
            vv = vc[d][k_row:k_row + 2 * B, :]
        qs = jnp.concatenate([jnp.where(lo, qp, zero_bf), jnp.where(lo, zero_bf, qp)], axis=0)
        s = lax.dot_general(qs, kk, (((1,), (1,)), ((), ())), preferred_element_type=F32)
        s = s + jnp.concatenate([bias, bias], axis=0)
        m = jnp.max(s, axis=-1, keepdims=True)
        p = jnp.exp2(s - m).astype(BF16)
        pv = jnp.dot(p, jnp.concatenate([vv, ones_v], axis=1), preferred_element_type=F32)
        den = jnp.where(lo, pv[:B, LANES:], pv[B:, LANES:])
        acc_o[d][rows_out, :] = jnp.where(lo, pv[:B, :LANES], pv[B:, :LANES]) / den
        acc_l[d][rows_out, :] = jnp.where(lo, m[:B], m[B:]) + jnp.log2(den)

    for d in ATT_DILATIONS:
        n = T // d
        for r in range(d):
            for jb in range(n // B):
                q_src, q_row = (q_ref, jb * B) if d == 1 else (qd[d], r * n + jb * B)
                rows_out = pl.ds(r + d * B * jb, B, stride=d) if d > 1 else pl.ds(jb * B, B)
                block(q_src, q_row, d, r * (n + B) + jb * B, bias_first if jb == 0 else bias_band,
                      rows_out)

    for c in range(T // ATT_COMBINE_ROWS):
        rows = slice(c * ATT_COMBINE_ROWS, (c + 1) * ATT_COMBINE_ROWS)
        ls = [acc_l[d][rows, :] for d in ATT_DILATIONS]
        mx = functools.reduce(jnp.maximum, ls)
        es = [jnp.exp2(l - mx) for l in ls]
        num = functools.reduce(jnp.add, [e * acc_o[d][rows, :] for e, d in zip(es, ATT_DILATIONS)])
        o_ref[rows, :] = (num / functools.reduce(jnp.add, es)).astype(o_ref.dtype)

    kc[1][0:B, :] = k_ref[T - B:T, :]
    vc[1][0:B, :] = v_ref[T - B:T, :]
    for d in ATT_DILATIONS[1:]:
        n = T // d
        for r in range(d):
            row = r * (n + B)
            kc[d][row:row + B, :] = kc[d][row + n:row + n + B, :]
            vc[d][row:row + B, :] = vc[d][row + n:row + n + B, :]


def _late_chunk_spec(shape, steps, step_index):
    rows, cols = shape
    share = 1 if (rows // steps) % BF16_SUBLANES == 0 else 2
    assert rows % (steps // share) == 0 and (rows * share // steps) % BF16_SUBLANES == 0
    return pl.BlockSpec((rows * share // steps, cols), lambda *idx: (step_index(*idx) // share, 0))


def _attn(q, k, v, late, batch, seq):
    T = q.shape[0]
    tiles = seq // ATT_SUPER
    pairs = ATT_WIDTH // LANES
    spec = pl.BlockSpec((ATT_SUPER, LANES), lambda b, h, t: (b * tiles + t, h))
    late_specs = [_late_chunk_spec(w.shape, batch * pairs * tiles,
                                   lambda b, h, t: (b * pairs + h) * tiles + t) for w in late]
    rows = lambda n, dt: pltpu.VMEM((n, LANES), dt)
    kv_rows = [2 * ATT_BLOCK] + [ATT_SUPER + d * ATT_BLOCK for d in ATT_DILATIONS[1:]]
    scratch = ([rows(ATT_SUPER, F32)] * 2 + [rows(ATT_SUPER, BF16) for _ in ATT_DILATIONS[1:]]
               + [rows(n, BF16) for n in kv_rows] + [rows(n, BF16) for n in kv_rows]
               + [rows(ATT_SUPER, F32) for _ in range(2 * len(ATT_DILATIONS))])
    res = pl.pallas_call(
        functools.partial(_attn_kernel, len(late)), grid=(batch, pairs, tiles),
        in_specs=[spec, spec, spec] + late_specs, out_specs=[spec] + late_specs,
        out_shape=[jax.ShapeDtypeStruct((T, ATT_WIDTH), BF16)]
        + [jax.ShapeDtypeStruct(w.shape, BF16) for w in late],
        scratch_shapes=scratch,
        compiler_params=pltpu.CompilerParams(
            dimension_semantics=("arbitrary", "arbitrary", "arbitrary"),
            vmem_limit_bytes=VMEM_LIMIT_BYTES),
        name="attn")(q, k, v, *late)
    return res[0], res[1:]


def _out_ffn_kernel(x1_ref, og_ref, oa_ref, ga_ref, wo_ref, g2_ref, w1_ref, w3_ref, w2_ref, gf_ref,
                    out_ref):
    groups = [slice(r, r + ROW_GROUP) for r in range(0, OUT_TOKEN_TILE, ROW_GROUP)]
    x2s = []
    for rows in groups:
        oa = _rms(oa_ref[rows, :].astype(F32), ga_ref[...]).astype(BF16)
        x2s.append(x1_ref[rows, :]
                   + jnp.dot(og_ref[rows, :], wo_ref[:GLA_WIDTH, :], preferred_element_type=F32)
                   + jnp.dot(oa, wo_ref[GLA_WIDTH:, :], preferred_element_type=F32))
    for rows, x2 in zip(groups, x2s):
        h = _rms(x2, g2_ref[...]).astype(BF16)
        x3 = x2 + 0.5 * _swiglu_half_step(h, w1_ref, w3_ref, w2_ref)
        out_ref[rows, :] = _rms(x3, gf_ref[...])


def _out_ffn(x1, og, oa, ga, wo, g2, w1, w3, w2, gf):
    T = x1.shape[0]
    tm = OUT_TOKEN_TILE
    row = lambda n: pl.BlockSpec((tm, n), lambda i: (i, 0))
    return pl.pallas_call(
        _out_ffn_kernel, grid=(T // tm,),
        in_specs=[row(D_MODEL), row(GLA_WIDTH), row(ATT_WIDTH), _const_spec(ga.shape),
                  _const_spec(wo.shape), _const_spec(g2.shape),
                  _const_spec(w1.shape), _const_spec(w3.shape), _const_spec(w2.shape),
                  _const_spec(gf.shape)],
        out_specs=row(D_MODEL),
        out_shape=jax.ShapeDtypeStruct((T, D_MODEL), F32),
        compiler_params=pltpu.CompilerParams(dimension_semantics=("arbitrary",),
                                             vmem_limit_bytes=VMEM_LIMIT_BYTES),
        name="out_ffn")(x1, og, oa, ga, wo, g2, w1, w3, w2, gf)


def _rope_tables():
    j = np.arange(LANES) % ATT_HD
    half = ROT_DIM // 2
    inv_freq = ROPE_THETA ** (-np.arange(0, ROT_DIM, 2, dtype=np.float32) / ROT_DIM)
    freq = np.where(j < ROT_DIM, inv_freq[j % half], 0.0).astype(np.float32)
    sgn1 = np.where(j < half, -1.0, 0.0).astype(np.float32)
    sgn2 = np.where((j >= half) & (j < ROT_DIM), 1.0, 0.0).astype(np.float32)
    return freq[None, :], sgn1[None, :], sgn2[None, :]


def _chunk_tril():
    i = np.arange(GLA_HALF)
    same = (i[:, None] // GLA_CHUNK) == (i[None, :] // GLA_CHUNK)
    return jnp.asarray((same & (i[None, :] <= i[:, None])).astype(np.float32), dtype=BF16)


def kernel(x, positions, ffn1_norm, ffn1_w1, ffn1_w3, ffn1_w2, mix_norm, w_in, gla_w_a2, gla_b_a,
           gla_out_norm, att_out_norm, w_out, ffn2_norm, ffn2_w1, ffn2_w3, ffn2_w2, final_norm):
    batch, seq, _ = x.shape
    T = batch * seq
    depth = ffn1_norm.shape[0]
    assert depth == 1, "the final norm is fused into the single layer's last kernel"
    freq, sgn1, sgn2 = (jnp.asarray(t) for t in _rope_tables())
    ltri = _chunk_tril()
    pos = positions.astype(F32).reshape(T // TOKEN_TILE, 1, TOKEN_TILE)
    xs = x.reshape(T, D_MODEL)

    o_gla_end = 2 * GLA_KW + 2 * GLA_WIDTH
    o_ga_end = o_gla_end + GLA_RANK
    for l in range(depth):
        wi = w_in[l]
        wgla = wi[:, :o_gla_end].astype(BF16)
        wga = jnp.pad(wi[:, o_gla_end:o_ga_end].astype(BF16), ((0, 0), (0, LANES - GLA_RANK)))
        watt = wi[:, o_ga_end:].astype(BF16)
        wa2 = jnp.pad(gla_w_a2[l].astype(BF16), ((0, LANES - GLA_RANK), (0, 0)))
        x1, gq, gk, gv, gr, la, aq, ak, av = _ffn_proj(
            xs, pos, ffn1_norm[l][None, :], ffn1_w1[l].astype(BF16), ffn1_w3[l].astype(BF16),
            ffn1_w2[l].astype(BF16), mix_norm[l][None, :], wgla, wga, wa2, gla_b_a[l][None, :], watt,
            freq, sgn1, sgn2)

        o_gla = _gla(gq, gk, gv, la, gr, gla_out_norm[l][None, :], ltri, batch)

        o_att, (w1b, w3b, w2b, wob) = _attn(
            aq, ak, av, (ffn2_w1[l], ffn2_w3[l], ffn2_w2[l], w_out[l]), batch, seq)

        xs = _out_ffn(x1, o_gla, o_att, att_out_norm[l][None, :], wob, ffn2_norm[l][None, :], w1b, w3b,
                      w2b, final_norm[None, :])
    return xs.reshape(batch, seq, D_MODEL)
```

```python
import functools
import math

import jax
import jax.numpy as jnp
import numpy as np
from jax import lax
from jax.experimental import pallas as pl
from jax.experimental.pallas import tpu as pltpu

F32 = jnp.float32
BF16 = jnp.bfloat16

D_MODEL = 1024
D_FF = 2816
GLA_WIDTH = 512
GLA_HEADS = 4
GLA_DV = 128
GLA_DK = 64
GLA_KW = GLA_HEADS * GLA_DK
GLA_RANK = 16
GLA_TAU = 16.0
GLA_CHUNK = 64
ATT_WIDTH = 512
ATT_HEADS = 8
ATT_HD = 64
ROT_DIM = 16
ROPE_THETA = 500000.0
DILATED_PATTERNS = ((128, 1), (512, 4), (2048, 16))
ATT_BLOCK = 128
EPS = 1e-6

LANES = 128
SUBLANES = 8
BF16_SUBLANES = 16
V7X_VMEM_BYTES = 64 * 1024 * 1024
VMEM_LIMIT_BYTES = V7X_VMEM_BYTES * 7 // 8
FFN_PROJ_VMEM_LIMIT_BYTES = V7X_VMEM_BYTES * 15 // 16

TOKEN_TILE = 1024
OUT_TOKEN_TILE = 1024
ROW_GROUP = 256
MXU_TILE = 256
FF_SPLITS = (0, (D_FF // MXU_TILE // 2) * MXU_TILE, D_FF)
assert D_FF % MXU_TILE == 0
GLA_TILE = 2048
GLA_HALF = 256
ATT_DILATIONS = tuple(sorted(d for _, d in DILATED_PATTERNS))
ATT_SUPER = max(ATT_DILATIONS) * ATT_BLOCK
ATT_COMBINE_ROWS = 256
NEG = -1e30
LOG2E = math.log2(math.e)

assert all(w // d == ATT_BLOCK for w, d in DILATED_PATTERNS)
assert ATT_DILATIONS[0] == 1 and all(ATT_SUPER % (d * ATT_BLOCK) == 0 for d in ATT_DILATIONS)
assert ATT_DILATIONS == (1, 4, 16)


def _const_spec(shape):
    nd = len(shape)
    return pl.BlockSpec(shape, lambda *_: (0,) * nd, pipeline_mode=pl.Buffered(1))


def _rms(x, g):
    return x * lax.rsqrt(jnp.mean(x * x, axis=-1, keepdims=True) + EPS) * g


def _swiglu_half_step(h, w1_ref, w3_ref, w2_ref):
    acc = None
    for lo, hi in zip(FF_SPLITS[:-1], FF_SPLITS[1:]):
        sl = slice(lo, hi)
        a = jnp.dot(h, w1_ref[:, sl], preferred_element_type=F32)
        b = jnp.dot(h, w3_ref[:, sl], preferred_element_type=F32)
        g = (a / (1.0 + jnp.exp(-a)) * b).astype(BF16)
        part = jnp.dot(g, w2_ref[sl, :], preferred_element_type=F32)
        acc = part if acc is None else acc + part
    return acc


def _ffn_proj_kernel(x_ref, pos_ref, g1_ref, w1_ref, w3_ref, w2_ref, gmix_ref, wgla_ref, wga_ref,
                     wa2_ref, ba_ref, watt_ref, freq_ref, sgn1_ref, sgn2_ref,
                     x1_ref, gq_ref, gk_ref, gv_ref, gr_ref, la_ref, aq_ref, ak_ref, av_ref):
    groups = [slice(r, r + ROW_GROUP) for r in range(0, TOKEN_TILE, ROW_GROUP)]
    x1s = []
    for rows in groups:
        x = x_ref[rows, :]
        h = _rms(x, g1_ref[...]).astype(BF16)
        x1 = x + 0.5 * _swiglu_half_step(h, w1_ref, w3_ref, w2_ref)
        x1_ref[rows, :] = x1
        x1s.append(x1)

    pos_col = jnp.broadcast_to(pos_ref[...], (SUBLANES, TOKEN_TILE)).T[:, 0:1]
    for rows, x1 in zip(groups, x1s):
        h2 = _rms(x1, gmix_ref[...]).astype(BF16)
        pa = jnp.dot(h2, watt_ref[...], preferred_element_type=F32)
        av_ref[rows, :] = pa[:, 2 * ATT_WIDTH:].astype(BF16)

        ang = pos_col[rows, :] * freq_ref[...]
        cos = jnp.cos(ang)
        sin = jnp.sin(ang)
        s1 = sin * sgn1_ref[...]
        s2 = sin * sgn2_ref[...]
        half = ROT_DIM // 2
        for off, ref, scale in ((0, aq_ref, ATT_HD ** -0.5 * LOG2E), (ATT_WIDTH, ak_ref, 1.0)):
            for cg in range(ATT_WIDTH // LANES):
                t = pa[:, off + cg * LANES: off + (cg + 1) * LANES]
                r = t * cos + pltpu.roll(t, LANES - half, 1) * s1 + pltpu.roll(t, half, 1) * s2
                ref[rows, cg * LANES:(cg + 1) * LANES] = (r * scale).astype(BF16)

        ga = jnp.dot(h2, wga_ref[...], preferred_element_type=F32).astype(BF16)
        z = jnp.dot(ga, wa2_ref[...], preferred_element_type=F32) + ba_ref[...]
        la_ref[rows, :] = (jnp.minimum(z, 0.0) - jnp.log(1.0 + jnp.exp(-jnp.abs(z)))) * (1.0 / GLA_TAU)

        pg = jnp.dot(h2, wgla_ref[...], preferred_element_type=F32)
        gq_ref[rows, :] = (pg[:, :GLA_KW] * (GLA_DK ** -0.5)).astype(BF16)
        gk_ref[rows, :] = pg[:, GLA_KW:2 * GLA_KW].astype(BF16)
        gv_ref[rows, :] = pg[:, 2 * GLA_KW:2 * GLA_KW + GLA_WIDTH].astype(BF16)
        gr = pg[:, 2 * GLA_KW + GLA_WIDTH:]
        gr_ref[rows, :] = (gr / (1.0 + jnp.exp(-gr))).astype(BF16)


def _ffn_proj(x2d, pos, g1, w1, w3, w2, gmix, wgla, wga, wa2, ba, watt, freq, sgn1, sgn2):
    T = x2d.shape[0]
    tm = TOKEN_TILE
    row = lambda n: pl.BlockSpec((tm, n), lambda i: (i, 0))
    pos_spec = pl.BlockSpec((None, 1, tm), lambda i: (i, 0, 0))
    in_specs = [row(D_MODEL), pos_spec, _const_spec(g1.shape), _const_spec(w1.shape), _const_spec(w3.shape),
                _const_spec(w2.shape), _const_spec(gmix.shape), _const_spec(wgla.shape),
                _const_spec(wga.shape), _const_spec(wa2.shape), _const_spec(ba.shape),
                _const_spec(watt.shape), _const_spec(freq.shape), _const_spec(sgn1.shape),
                _const_spec(sgn2.shape)]
    out_shape = [jax.ShapeDtypeStruct((T, D_MODEL), F32),
                 jax.ShapeDtypeStruct((T, GLA_KW), BF16), jax.ShapeDtypeStruct((T, GLA_KW), BF16),
                 jax.ShapeDtypeStruct((T, GLA_WIDTH), BF16), jax.ShapeDtypeStruct((T, GLA_WIDTH), BF16),
                 jax.ShapeDtypeStruct((T, GLA_KW), F32),
                 jax.ShapeDtypeStruct((T, ATT_WIDTH), BF16), jax.ShapeDtypeStruct((T, ATT_WIDTH), BF16),
                 jax.ShapeDtypeStruct((T, ATT_WIDTH), BF16)]
    return pl.pallas_call(
        _ffn_proj_kernel, grid=(T // tm,), in_specs=in_specs,
        out_specs=[row(s.shape[1]) for s in out_shape], out_shape=out_shape,
        compiler_params=pltpu.CompilerParams(dimension_semantics=("arbitrary",),
                                             vmem_limit_bytes=FFN_PROJ_VMEM_LIMIT_BYTES),
        name="ffn_proj")(x2d, pos, g1, w1, w3, w2, gmix, wgla, wga, wa2, ba, watt, freq, sgn1, sgn2)


def _gla_kernel(q_ref, k_ref, v_ref, la_ref, r_ref, gn_ref, ltri_ref, o_ref, s_ref):
    @pl.when(pl.program_id(1) == 0)
    def _():
        s_ref[...] = jnp.zeros_like(s_ref)

    C = GLA_CHUNK
    r_kk = lax.broadcasted_iota(jnp.int32, (GLA_KW, GLA_KW), 0) // C
    c_kk = lax.broadcasted_iota(jnp.int32, (GLA_KW, GLA_KW), 1) // C
    bd_k = r_kk == c_kk
    r_kv = lax.broadcasted_iota(jnp.int32, (GLA_KW, GLA_WIDTH), 0) // C
    c_kv = lax.broadcasted_iota(jnp.int32, (GLA_KW, GLA_WIDTH), 1) // GLA_DV
    bd_v = r_kv == c_kv
    causal = (lax.broadcasted_iota(jnp.int32, (C, GLA_KW), 1) % C
              <= lax.broadcasted_iota(jnp.int32, (C, GLA_KW), 0))
    ltri = ltri_ref[...]

    for hf in range(GLA_TILE // GLA_HALF):
        base = hf * GLA_HALF
        g = la_ref[base:base + GLA_HALF, :]
        g_hi = g.astype(BF16)
        g_lo = (g - g_hi.astype(F32)).astype(BF16)
        b = (jnp.dot(ltri, g_hi, preferred_element_type=F32)
             + jnp.dot(ltri, g_lo, preferred_element_type=F32))
        for c in range(GLA_HALF // C):
            lo = base + c * C
            bc = b[c * C:(c + 1) * C, :]
            bl = bc[C - 1:C, :]
            qc = q_ref[lo:lo + C, :].astype(F32)
            kc = k_ref[lo:lo + C, :].astype(F32)
            vc = v_ref[lo:lo + C, :]
            qd = (qc * jnp.exp(bc)).astype(BF16)
            k_inv = kc * jnp.exp(-bc)
            ki = k_inv.astype(BF16)
            kt = k_inv * jnp.exp(bl)
            kbd = jnp.where(bd_k, jnp.tile(ki, (GLA_HEADS, 1)), jnp.zeros((), BF16))
            a = lax.dot_general(qd, kbd, (((1,), (1,)), ((), ())), preferred_element_type=F32)
            a = jnp.where(causal, a, 0.0).astype(BF16)
            vbd = jnp.where(bd_v, jnp.tile(vc, (GLA_HEADS, 1)), jnp.zeros((), BF16))
            s_prev = s_ref[...]
            sbd = jnp.where(bd_v, jnp.tile(s_prev.astype(BF16), (1, GLA_HEADS)), jnp.zeros((), BF16))
            o = (jnp.dot(a, vbd, preferred_element_type=F32)
                 + jnp.dot(qd, sbd, preferred_element_type=F32))

            kt_t = kt.T.astype(BF16)
            dec = jnp.exp(bc.T[:, C - 1:C])
            u = jnp.concatenate(
                [jnp.dot(kt_t[h * GLA_DK:(h + 1) * GLA_DK, :], vc[:, h * GLA_DV:(h + 1) * GLA_DV],
                         preferred_element_type=F32) for h in range(GLA_HEADS)], axis=0)
            s_ref[...] = dec * s_prev + u

            gate = r_ref[lo:lo + C, :].astype(F32)
            for h in range(GLA_HEADS):
                hs = slice(h * GLA_DV, (h + 1) * GLA_DV)
                oh = _rms(o[:, hs], gn_ref[:, hs])
                o_ref[lo:lo + C, hs] = (oh * gate[:, hs]).astype(BF16)


def _gla(gq, gk, gv, la, gr, gn, ltri, batch):
    T = gq.shape[0]
    tiles = T // batch // GLA_TILE
    row = lambda n: pl.BlockSpec((GLA_TILE, n), lambda b, t: (b * tiles + t, 0))
    return pl.pallas_call(
        _gla_kernel, grid=(batch, tiles),
        in_specs=[row(GLA_KW), row(GLA_KW), row(GLA_WIDTH), row(GLA_KW), row(GLA_WIDTH),
                  _const_spec(gn.shape), _const_spec(ltri.shape)],
        out_specs=row(GLA_WIDTH),
        out_shape=jax.ShapeDtypeStruct((T, GLA_WIDTH), BF16),
        scratch_shapes=[pltpu.VMEM((GLA_KW, GLA_DV), F32)],
        compiler_params=pltpu.CompilerParams(dimension_semantics=("arbitrary", "arbitrary"),
                                             vmem_limit_bytes=VMEM_LIMIT_BYTES),
        name="gla")(gq, gk, gv, la, gr, gn, ltri)


def _attn_kernel(n_late, q_ref, k_ref, v_ref, *rest):
    late_in, o_ref, late_out = rest[:n_late], rest[n_late], rest[n_late + 1:2 * n_late + 1]
    xf_ref, x4_ref, *scratch = rest[2 * n_late + 1:]
    for src, dst in zip(late_in, late_out):
        dst[...] = src[...].astype(BF16)

    B = ATT_BLOCK
    T = ATT_SUPER
    nd = len(ATT_DILATIONS)
    qd = dict(zip(ATT_DILATIONS[1:], scratch[:nd - 1]))
    kc = dict(zip(ATT_DILATIONS, scratch[nd - 1:2 * nd - 1]))
    vc = dict(zip(ATT_DILATIONS, scratch[2 * nd - 1:3 * nd - 1]))
    acc = scratch[3 * nd - 1:]
    acc_o = dict(zip(ATT_DILATIONS, acc[0::2]))
    acc_l = dict(zip(ATT_DILATIONS, acc[1::2]))
    tile = pl.program_id(2)

    def stage(src_ref, dst, halo):
        xf_ref[...] = src_ref[...].astype(F32)
        n4 = T // 4
        for r in range(4):
            c4 = xf_ref[pl.ds(r, n4, stride=4), :]
            x4_ref[r * n4:(r + 1) * n4, :] = c4
            lo_row = r * (n4 + halo) + halo
            dst[4][lo_row:lo_row + n4, :] = c4.astype(BF16)
        n16 = T // 16
        for r in range(16):
            lo_row = r * (n16 + halo) + halo
            dst[16][lo_row:lo_row + n16, :] = (
                x4_ref[pl.ds((r % 4) * n4 + r // 4, n16, stride=4), :].astype(BF16))

    @pl.when(tile == 0)
    def _():
        zeros = jnp.zeros((B, LANES), BF16)
        for d in ATT_DILATIONS:
            for r in range(d):
                row = r * (T // d + B)
                kc[d][row:row + B, :] = zeros
                vc[d][row:row + B, :] = zeros

    stage(q_ref, qd, 0)
    stage(k_ref, kc, B)
    stage(v_ref, vc, B)
    kc[1][B:, :] = k_ref[...]
    vc[1][B:, :] = v_ref[...]

    qi = lax.broadcasted_iota(jnp.int32, (B, 2 * B), 0)
    ki = lax.broadcasted_iota(jnp.int32, (B, 2 * B), 1)
    band = (ki >= qi) & (ki <= qi + B)
    bias_band = jnp.where(band, 0.0, NEG).astype(F32)
    bias_first = jnp.where(band & ((ki >= B) | (tile > 0)), 0.0, NEG).astype(F32)
    lane = lax.broadcasted_iota(jnp.int32, (B, LANES), 1)
    lo = lane < ATT_HD
    zero_bf = jnp.zeros((), BF16)
    ones_v = jnp.ones((2 * B, LANES), BF16)

    def block(q_src, q_row, d, k_row, bias, rows_out):
        qp = q_src[q_row:q_row + B, :]
        kk = kc[d][k_row:k_row + 2 * B, :]
        vv = vc[d][k_row:k_row + 2 * B, :]
        qs = jnp.concatenate([jnp.where(lo, qp, zero_bf), jnp.where(lo, zero_bf, qp)], axis=0)
        s = lax.dot_general(qs, kk, (((1,), (1,)), ((), ())), preferred_element_type=F32)
        s = s + jnp.concatenate([bias, bias], axis=0)
        m = jnp.max(s, axis=-1, keepdims=True)
        p = jnp.exp2(s - m).astype(BF16)
        pv = jnp.dot(p, jnp.concatenate([vv, ones_v], axis=1), preferred_element_type=F32)
        den = jnp.where(lo, pv[:B, LANES:], pv[B:, LANES:])
        acc_o[d][rows_out, :] = jnp.where(lo, pv[:B, :LANES], pv[B:, :LANES]) / den
        acc_l[d][rows_out, :] = jnp.where(lo, m[:B], m[B:]) + jnp.log2(den)

    for d in ATT_DILATIONS:
        n = T // d
        for r in range(d):
            for jb in range(n // B):
                q_src, q_row = (q_ref, jb * B) if d == 1 else (qd[d], r * n + jb * B)
                rows_out = pl.ds(r + d * B * jb, B, stride=d) if d > 1 else pl.ds(jb * B, B)
                block(q_src, q_row, d, r * (n + B) + jb * B, bias_first if jb == 0 else bias_band,
                      rows_out)

    for c in range(T // ATT_COMBINE_ROWS):
        rows = slice(c * ATT_COMBINE_ROWS, (c + 1) * ATT_COMBINE_ROWS)
        ls = [acc_l[d][rows, :] for d in ATT_DILATIONS]
        mx = functools.reduce(jnp.maximum, ls)
        es = [jnp.exp2(l - mx) for l in ls]
        num = functools.reduce(jnp.add, [e * acc_o[d][rows, :] for e, d in zip(es, ATT_DILATIONS)])
        o_ref[rows, :] = (num / functools.reduce(jnp.add, es)).astype(o_ref.dtype)

    for d in ATT_DILATIONS:
        n = T // d
        for r in range(d):
            row = r * (n + B)
            kc[d][row:row + B, :] = kc[d][row + n:row + n + B, :]
            vc[d][row:row + B, :] = vc[d][row + n:row + n + B, :]


def _late_chunk_spec(shape, steps, step_index):
    rows, cols = shape
    share = 1 if (rows // steps) % BF16_SUBLANES == 0 else 2
    assert rows % (steps // share) == 0 and (rows * share // steps) % BF16_SUBLANES == 0
    return pl.BlockSpec((rows * share // steps, cols), lambda *idx: (step_index(*idx) // share, 0))


def _attn(q, k, v, late, batch, seq):
    T = q.shape[0]
    tiles = seq // ATT_SUPER
    pairs = ATT_WIDTH // LANES
    spec = pl.BlockSpec((ATT_SUPER, LANES), lambda b, h, t: (b * tiles + t, h))
    late_specs = [_late_chunk_spec(w.shape, batch * pairs * tiles,
                                   lambda b, h, t: (b * pairs + h) * tiles + t) for w in late]
    rows = lambda n, dt: pltpu.VMEM((n, LANES), dt)
    kv_rows = [ATT_SUPER + d * ATT_BLOCK for d in ATT_DILATIONS]
    scratch = ([rows(ATT_SUPER, F32)] * 2 + [rows(ATT_SUPER, BF16) for _ in ATT_DILATIONS[1:]]
               + [rows(n, BF16) for n in kv_rows] + [rows(n, BF16) for n in kv_rows]
               + [rows(ATT_SUPER, F32) for _ in range(2 * len(ATT_DILATIONS))])
    res = pl.pallas_call(
        functools.partial(_attn_kernel, len(late)), grid=(batch, pairs, tiles),
        in_specs=[spec, spec, spec] + late_specs, out_specs=[spec] + late_specs,
        out_shape=[jax.ShapeDtypeStruct((T, ATT_WIDTH), BF16)]
        + [jax.ShapeDtypeStruct(w.shape, BF16) for w in late],
        scratch_shapes=scratch,
        compiler_params=pltpu.CompilerParams(
            dimension_semantics=("arbitrary", "arbitrary", "arbitrary"),
            vmem_limit_bytes=VMEM_LIMIT_BYTES),
        name="attn")(q, k, v, *late)
    return res[0], res[1:]


def _out_ffn_kernel(x1_ref, og_ref, oa_ref, ga_ref, wo_ref, g2_ref, w1_ref, w3_ref, w2_ref, gf_ref,
                    out_ref):
    groups = [slice(r, r + ROW_GROUP) for r in range(0, OUT_TOKEN_TILE, ROW_GROUP)]
    x2s = []
    for rows in groups:
        oa = _rms(oa_ref[rows, :].astype(F32), ga_ref[...]).astype(BF16)
        x2s.append(x1_ref[rows, :]
                   + jnp.dot(og_ref[rows, :], wo_ref[:GLA_WIDTH, :], preferred_element_type=F32)
                   + jnp.dot(oa, wo_ref[GLA_WIDTH:, :], preferred_element_type=F32))
    for rows, x2 in zip(groups, x2s):
        h = _rms(x2, g2_ref[...]).astype(BF16)
        x3 = x2 + 0.5 * _swiglu_half_step(h, w1_ref, w3_ref, w2_ref)
        out_ref[rows, :] = _rms(x3, gf_ref[...])


def _out_ffn(x1, og, oa, ga, wo, g2, w1, w3, w2, gf):
    T = x1.shape[0]
    tm = OUT_TOKEN_TILE
    row = lambda n: pl.BlockSpec((tm, n), lambda i: (i, 0))
    return pl.pallas_call(
        _out_ffn_kernel, grid=(T // tm,),
        in_specs=[row(D_MODEL), row(GLA_WIDTH), row(ATT_WIDTH), _const_spec(ga.shape),
                  _const_spec(wo.shape), _const_spec(g2.shape),
                  _const_spec(w1.shape), _const_spec(w3.shape), _const_spec(w2.shape),
                  _const_spec(gf.shape)],
        out_specs=row(D_MODEL),
        out_shape=jax.ShapeDtypeStruct((T, D_MODEL), F32),
        compiler_params=pltpu.CompilerParams(dimension_semantics=("arbitrary",),
                                             vmem_limit_bytes=VMEM_LIMIT_BYTES),
        name="out_ffn")(x1, og, oa, ga, wo, g2, w1, w3, w2, gf)


def _rope_tables():
    j = np.arange(LANES) % ATT_HD
    half = ROT_DIM // 2
    inv_freq = ROPE_THETA ** (-np.arange(0, ROT_DIM, 2, dtype=np.float32) / ROT_DIM)
    freq = np.where(j < ROT_DIM, inv_freq[j % half], 0.0).astype(np.float32)
    sgn1 = np.where(j < half, -1.0, 0.0).astype(np.float32)
    sgn2 = np.where((j >= half) & (j < ROT_DIM), 1.0, 0.0).astype(np.float32)
    return freq[None, :], sgn1[None, :], sgn2[None, :]


def _chunk_tril():
    i = np.arange(GLA_HALF)
    same = (i[:, None] // GLA_CHUNK) == (i[None, :] // GLA_CHUNK)
    return jnp.asarray((same & (i[None, :] <= i[:, None])).astype(np.float32), dtype=BF16)


def kernel(x, positions, ffn1_norm, ffn1_w1, ffn1_w3, ffn1_w2, mix_norm, w_in, gla_w_a2, gla_b_a,
           gla_out_norm, att_out_norm, w_out, ffn2_norm, ffn2_w1, ffn2_w3, ffn2_w2, final_norm):
    batch, seq, _ = x.shape
    T = batch * seq
    depth = ffn1_norm.shape[0]
    assert depth == 1, "the final norm is fused into the single layer's last kernel"
    freq, sgn1, sgn2 = (jnp.asarray(t) for t in _rope_tables())
    ltri = _chunk_tril()
    pos = positions.astype(F32).reshape(T // TOKEN_TILE, 1, TOKEN_TILE)
    xs = x.reshape(T, D_MODEL)

    o_gla_end = 2 * GLA_KW + 2 * GLA_WIDTH
    o_ga_end = o_gla_end + GLA_RANK
    for l in range(depth):
        wi = w_in[l]
        wgla = wi[:, :o_gla_end].astype(BF16)
        wga = jnp.pad(wi[:, o_gla_end:o_ga_end].astype(BF16), ((0, 0), (0, LANES - GLA_RANK)))
        watt = wi[:, o_ga_end:].astype(BF16)
        wa2 = jnp.pad(gla_w_a2[l].astype(BF16), ((0, LANES - GLA_RANK), (0, 0)))
        x1, gq, gk, gv, gr, la, aq, ak, av = _ffn_proj(
            xs, pos, ffn1_norm[l][None, :], ffn1_w1[l].astype(BF16), ffn1_w3[l].astype(BF16),
            ffn1_w2[l].astype(BF16), mix_norm[l][None, :], wgla, wga, wa2, gla_b_a[l][None, :], watt,
            freq, sgn1, sgn2)

        o_gla = _gla(gq, gk, gv, la, gr, gla_out_norm[l][None, :], ltri, batch)

        o_att, (w1b, w3b, w2b, wob) = _attn(
            aq, ak, av, (ffn2_w1[l], ffn2_w3[l], ffn2_w2[l], w_out[l]), batch, seq)

        xs = _out_ffn(x1, o_gla, o_att, att_out_norm[l][None, :], wob, ffn2_norm[l][None, :], w1b, w3b,
                      w2b, final_norm[None, :])
    return xs.reshape(batch, seq, D_MODEL)
```

```python
import functools
import math

import jax
import jax.numpy as jnp
import numpy as np
from jax import lax
from jax.experimental import pallas as pl
from jax.experimental.pallas import tpu as pltpu

F32 = jnp.float32
BF16 = jnp.bfloat16

D_MODEL = 1024
D_FF = 2816
GLA_WIDTH = 512
GLA_HEADS = 4
GLA_DV = 128
GLA_DK = 64
GLA_KW = GLA_HEADS * GLA_DK
GLA_RANK = 16
GLA_TAU = 16.0
GLA_CHUNK = 64
ATT_WIDTH = 512
ATT_HEADS = 8
ATT_HD = 64
ROT_DIM = 16
ROPE_THETA = 500000.0
DILATED_PATTERNS = ((128, 1), (512, 4), (2048, 16))
ATT_BLOCK = 128
EPS = 1e-6

LANES = 128
SUBLANES = 8
BF16_SUBLANES = 16
V7X_VMEM_BYTES = 64 * 1024 * 1024
VMEM_LIMIT_BYTES = V7X_VMEM_BYTES * 7 // 8
FFN_PROJ_VMEM_LIMIT_BYTES = V7X_VMEM_BYTES * 15 // 16

TOKEN_TILE = 1024
OUT_TOKEN_TILE = 1024
ROW_GROUP = 256
MXU_TILE = 256
FF_SPLITS = (0, (D_FF // MXU_TILE // 2) * MXU_TILE, D_FF)
assert D_FF % MXU_TILE == 0
GLA_TILE = 2048
GLA_HALF = 256
ATT_DILATIONS = tuple(sorted(d for _, d in DILATED_PATTERNS))
ATT_SUPER = max(ATT_DILATIONS) * ATT_BLOCK
ATT_COMBINE_ROWS = 256
NEG = -1e30
LOG2E = math.log2(math.e)

assert all(w // d == ATT_BLOCK for w, d in DILATED_PATTERNS)
assert ATT_DILATIONS[0] == 1 and all(ATT_SUPER % (d * ATT_BLOCK) == 0 for d in ATT_DILATIONS)
assert ATT_DILATIONS == (1, 4, 16)


def _const_spec(shape):
    nd = len(shape)
    return pl.BlockSpec(shape, lambda *_: (0,) * nd, pipeline_mode=pl.Buffered(1))


def _rms(x, g):
    return x * lax.rsqrt(jnp.mean(x * x, axis=-1, keepdims=True) + EPS) * g


def _swiglu_half_step(h, w1_ref, w3_ref, w2_ref):
    acc = None
    for lo, hi in zip(FF_SPLITS[:-1], FF_SPLITS[1:]):
        sl = slice(lo, hi)
        a = jnp.dot(h, w1_ref[:, sl], preferred_element_type=F32)
        b = jnp.dot(h, w3_ref[:, sl], preferred_element_type=F32)
        g = (a / (1.0 + jnp.exp(-a)) * b).astype(BF16)
        part = jnp.dot(g, w2_ref[sl, :], preferred_element_type=F32)
        acc = part if acc is None else acc + part
    return acc


def _ffn_proj_kernel(x_ref, pos_ref, g1_ref, w1_ref, w3_ref, w2_ref, gmix_ref, wgla_ref, wga_ref,
                     wa2_ref, ba_ref, watt_ref, freq_ref, sgn1_ref, sgn2_ref,
                     x1_ref, gq_ref, gk_ref, gv_ref, gr_ref, la_ref, aq_ref, ak_ref, av_ref):
    groups = [slice(r, r + ROW_GROUP) for r in range(0, TOKEN_TILE, ROW_GROUP)]
    x1s = []
    for rows in groups:
        x = x_ref[rows, :]
        h = _rms(x, g1_ref[...]).astype(BF16)
        x1 = x + 0.5 * _swiglu_half_step(h, w1_ref, w3_ref, w2_ref)
        x1_ref[rows, :] = x1
        x1s.append(x1)

    pos_col = jnp.broadcast_to(pos_ref[...], (SUBLANES, TOKEN_TILE)).T[:, 0:1]
    for rows, x1 in zip(groups, x1s):
        h2 = _rms(x1, gmix_ref[...]).astype(BF16)
        pa = jnp.dot(h2, watt_ref[...], preferred_element_type=F32)
        av_ref[rows, :] = pa[:, 2 * ATT_WIDTH:].astype(BF16)

        ang = pos_col[rows, :] * freq_ref[...]
        cos = jnp.cos(ang)
        sin = jnp.sin(ang)
        s1 = sin * sgn1_ref[...]
        s2 = sin * sgn2_ref[...]
        half = ROT_DIM // 2
        for off, ref, scale in ((0, aq_ref, ATT_HD ** -0.5 * LOG2E), (ATT_WIDTH, ak_ref, 1.0)):
            for cg in range(ATT_WIDTH // LANES):
                t = pa[:, off + cg * LANES: off + (cg + 1) * LANES]
                r = t * cos + pltpu.roll(t, LANES - half, 1) * s1 + pltpu.roll(t, half, 1) * s2
                ref[rows, cg * LANES:(cg + 1) * LANES] = (r * scale).astype(BF16)

        ga = jnp.dot(h2, wga_ref[...], preferred_element_type=F32).astype(BF16)
        z = jnp.dot(ga, wa2_ref[...], preferred_element_type=F32) + ba_ref[...]
        la_ref[rows, :] = (jnp.minimum(z, 0.0) - jnp.log(1.0 + jnp.exp(-jnp.abs(z)))) * (1.0 / GLA_TAU)

        pg = jnp.dot(h2, wgla_ref[...], preferred_element_type=F32)
        gq_ref[rows, :] = (pg[:, :GLA_KW] * (GLA_DK ** -0.5)).astype(BF16)
        gk_ref[rows, :] = pg[:, GLA_KW:2 * GLA_KW].astype(BF16)
        gv_ref[rows, :] = pg[:, 2 * GLA_KW:2 * GLA_KW + GLA_WIDTH].astype(BF16)
        gr = pg[:, 2 * GLA_KW + GLA_WIDTH:]
        gr_ref[rows, :] = (gr / (1.0 + jnp.exp(-gr))).astype(BF16)


def _ffn_proj(x2d, pos, g1, w1, w3, w2, gmix, wgla, wga, wa2, ba, watt, freq, sgn1, sgn2):
    T = x2d.shape[0]
    tm = TOKEN_TILE
    row = lambda n: pl.BlockSpec((tm, n), lambda i: (i, 0))
    pos_spec = pl.BlockSpec((None, 1, tm), lambda i: (i, 0, 0))
    in_specs = [row(D_MODEL), pos_spec, _const_spec(g1.shape), _const_spec(w1.shape), _const_spec(w3.shape),
                _const_spec(w2.shape), _const_spec(gmix.shape), _const_spec(wgla.shape),
                _const_spec(wga.shape), _const_spec(wa2.shape), _const_spec(ba.shape),
                _const_spec(watt.shape), _const_spec(freq.shape), _const_spec(sgn1.shape),
                _const_spec(sgn2.shape)]
    out_shape = [jax.ShapeDtypeStruct((T, D_MODEL), F32),
                 jax.ShapeDtypeStruct((T, GLA_KW), BF16), jax.ShapeDtypeStruct((T, GLA_KW), BF16),
                 jax.ShapeDtypeStruct((T, GLA_WIDTH), BF16), jax.ShapeDtypeStruct((T, GLA_WIDTH), BF16),
                 jax.ShapeDtypeStruct((T, GLA_KW), F32),
                 jax.ShapeDtypeStruct((T, ATT_WIDTH), BF16), jax.ShapeDtypeStruct((T, ATT_WIDTH), BF16),
                 jax.ShapeDtypeStruct((T, ATT_WIDTH), BF16)]
    return pl.pallas_call(
        _ffn_proj_kernel, grid=(T // tm,), in_specs=in_specs,
        out_specs=[row(s.shape[1]) for s in out_shape], out_shape=out_shape,
        compiler_params=pltpu.CompilerParams(dimension_semantics=("arbitrary",),
                                             vmem_limit_bytes=FFN_PROJ_VMEM_LIMIT_BYTES),
        name="ffn_proj")(x2d, pos, g1, w1, w3, w2, gmix, wgla, wga, wa2, ba, watt, freq, sgn1, sgn2)


def _gla_kernel(q_ref, k_ref, v_ref, la_ref, r_ref, gn_ref, ltri_ref, o_ref, s_ref):
    @pl.when(pl.program_id(1) == 0)
    def _():
        s_ref[...] = jnp.zeros_like(s_ref)

    C = GLA_CHUNK
    r_kk = lax.broadcasted_iota(jnp.int32, (GLA_KW, GLA_KW), 0) // C
    c_kk = lax.broadcasted_iota(jnp.int32, (GLA_KW, GLA_KW), 1) // C
    bd_k = r_kk == c_kk
    r_kv = lax.broadcasted_iota(jnp.int32, (GLA_KW, GLA_WIDTH), 0) // C
    c_kv = lax.broadcasted_iota(jnp.int32, (GLA_KW, GLA_WIDTH), 1) // GLA_DV
    bd_v = r_kv == c_kv
    causal = (lax.broadcasted_iota(jnp.int32, (C, GLA_KW), 1) % C
              <= lax.broadcasted_iota(jnp.int32, (C, GLA_KW), 0))
    ltri = ltri_ref[...]

    for hf in range(GLA_TILE // GLA_HALF):
        base = hf * GLA_HALF
        g = la_ref[base:base + GLA_HALF, :]
        g_hi = g.astype(BF16)
        g_lo = (g - g_hi.astype(F32)).astype(BF16)
        b = jnp.dot(jnp.concatenate([ltri, ltri], axis=1), jnp.concatenate([g_hi, g_lo], axis=0),
                    preferred_element_type=F32)
        for c in range(GLA_HALF // C):
            lo = base + c * C
            bc = b[c * C:(c + 1) * C, :]
            bl = bc[C - 1:C, :]
            qc = q_ref[lo:lo + C, :].astype(F32)
            kc = k_ref[lo:lo + C, :].astype(F32)
            vc = v_ref[lo:lo + C, :]
            qd = (qc * jnp.exp(bc)).astype(BF16)
            k_inv = kc * jnp.exp(-bc)
            ki = k_inv.astype(BF16)
            kt = k_inv * jnp.exp(bl)
            kbd = jnp.where(bd_k, jnp.tile(ki, (GLA_HEADS, 1)), jnp.zeros((), BF16))
            a = lax.dot_general(qd, kbd, (((1,), (1,)), ((), ())), preferred_element_type=F32)
            a = jnp.where(causal, a, 0.0).astype(BF16)
            vbd = jnp.where(bd_v, jnp.tile(vc, (GLA_HEADS, 1)), jnp.zeros((), BF16))
            s_prev = s_ref[...]
            sbd = jnp.where(bd_v, jnp.tile(s_prev.astype(BF16), (1, GLA_HEADS)), jnp.zeros((), BF16))
            o = (jnp.dot(a, vbd, preferred_element_type=F32)
                 + jnp.dot(qd, sbd, preferred_element_type=F32))

            kt_t = kt.T.astype(BF16)
            dec = jnp.exp(bc.T[:, C - 1:C])
            u = jnp.concatenate(
                [jnp.dot(kt_t[h * GLA_DK:(h + 1) * GLA_DK, :], vc[:, h * GLA_DV:(h + 1) * GLA_DV],
                         preferred_element_type=F32) for h in range(GLA_HEADS)], axis=0)
            s_ref[...] = dec * s_prev + u

            gate = r_ref[lo:lo + C, :].astype(F32)
            for h in range(GLA_HEADS):
                hs = slice(h * GLA_DV, (h + 1) * GLA_DV)
                oh = _rms(o[:, hs], gn_ref[:, hs])
                o_ref[lo:lo + C, hs] = (oh * gate[:, hs]).astype(BF16)


def _gla(gq, gk, gv, la, gr, gn, ltri, batch):
    T = gq.shape[0]
    tiles = T // batch // GLA_TILE
    row = lambda n: pl.BlockSpec((GLA_TILE, n), lambda b, t: (b * tiles + t, 0))
    return pl.pallas_call(
        _gla_kernel, grid=(batch, tiles),
        in_specs=[row(GLA_KW), row(GLA_KW), row(GLA_WIDTH), row(GLA_KW), row(GLA_WIDTH),
                  _const_spec(gn.shape), _const_spec(ltri.shape)],
        out_specs=row(GLA_WIDTH),
        out_shape=jax.ShapeDtypeStruct((T, GLA_WIDTH), BF16),
        scratch_shapes=[pltpu.VMEM((GLA_KW, GLA_DV), F32)],
        compiler_params=pltpu.CompilerParams(dimension_semantics=("arbitrary", "arbitrary"),
                                             vmem_limit_bytes=VMEM_LIMIT_BYTES),
        name="gla")(gq, gk, gv, la, gr, gn, ltri)


def _attn_kernel(n_late, q_ref, k_ref, v_ref, *rest):
    late_in, o_ref, late_out = rest[:n_late], rest[n_late], rest[n_late + 1:2 * n_late + 1]
    xf_ref, x4_ref, *scratch = rest[2 * n_late + 1:]
    for src, dst in zip(late_in, late_out):
        dst[...] = src[...].astype(BF16)

    B = ATT_BLOCK
    T = ATT_SUPER
    nd = len(ATT_DILATIONS)
    qd = dict(zip(ATT_DILATIONS[1:], scratch[:nd - 1]))
    kc = dict(zip(ATT_DILATIONS, scratch[nd - 1:2 * nd - 1]))
    vc = dict(zip(ATT_DILATIONS, scratch[2 * nd - 1:3 * nd - 1]))
    acc = scratch[3 * nd - 1:]
    acc_o = dict(zip(ATT_DILATIONS, acc[0::2]))
    acc_l = dict(zip(ATT_DILATIONS, acc[1::2]))
    tile = pl.program_id(2)

    def stage(src_ref, dst, halo):
        xf_ref[...] = src_ref[...].astype(F32)
        n4 = T // 4
        for r in range(4):
            c4 = xf_ref[pl.ds(r, n4, stride=4), :]
            x4_ref[r * n4:(r + 1) * n4, :] = c4
            lo_row = r * (n4 + halo) + halo
            dst[4][lo_row:lo_row + n4, :] = c4.astype(BF16)
        n16 = T // 16
        for r in range(16):
            lo_row = r * (n16 + halo) + halo
            dst[16][lo_row:lo_row + n16, :] = (
                x4_ref[pl.ds((r % 4) * n4 + r // 4, n16, stride=4), :].astype(BF16))

    @pl.when(tile == 0)
    def _():
        zeros = jnp.zeros((B, LANES), BF16)
        for d in ATT_DILATIONS:
            for r in range(d):
                row = r * (T // d + B)
                kc[d][row:row + B, :] = zeros
                vc[d][row:row + B, :] = zeros

    stage(q_ref, qd, 0)
    stage(k_ref, kc, B)
    stage(v_ref, vc, B)
    kc[1][B:, :] = k_ref[...]
    vc[1][B:, :] = v_ref[...]

    qi = lax.broadcasted_iota(jnp.int32, (B, 2 * B), 0)
    ki = lax.broadcasted_iota(jnp.int32, (B, 2 * B), 1)
    band = (ki >= qi) & (ki <= qi + B)
    bias_band = jnp.where(band, 0.0, NEG).astype(F32)
    bias_first = jnp.where(band & ((ki >= B) | (tile > 0)), 0.0, NEG).astype(F32)
    lane = lax.broadcasted_iota(jnp.int32, (B, LANES), 1)
    lo = lane < ATT_HD
    zero_bf = jnp.zeros((), BF16)
    ones_v = jnp.ones((2 * B, LANES), BF16)

    def block(q_src, q_row, d, k_row, bias, rows_out):
        qp = q_src[q_row:q_row + B, :]
        kk = kc[d][k_row:k_row + 2 * B, :]
        vv = vc[d][k_row:k_row + 2 * B, :]
        qs = jnp.concatenate([jnp.where(lo, qp, zero_bf), jnp.where(lo, zero_bf, qp)], axis=0)
        s = lax.dot_general(qs, kk, (((1,), (1,)), ((), ())), preferred_element_type=F32)
        s = s + jnp.concatenate([bias, bias], axis=0)
        m = jnp.max(s, axis=-1, keepdims=True)
        p = jnp.exp2(s - m).astype(BF16)
        pv = jnp.dot(p, jnp.concatenate([vv, ones_v], axis=1), preferred_element_type=F32)
        den = jnp.where(lo, pv[:B, LANES:], pv[B:, LANES:])
        acc_o[d][rows_out, :] = jnp.where(lo, pv[:B, :LANES], pv[B:, :LANES]) / den
        acc_l[d][rows_out, :] = jnp.where(lo, m[:B], m[B:]) + jnp.log2(den)

    for d in ATT_DILATIONS:
        n = T // d
        for r in range(d):
            for jb in range(n // B):
                q_src, q_row = (q_ref, jb * B) if d == 1 else (qd[d], r * n + jb * B)
                rows_out = pl.ds(r + d * B * jb, B, stride=d) if d > 1 else pl.ds(jb * B, B)
                block(q_src, q_row, d, r * (n + B) + jb * B, bias_first if jb == 0 else bias_band,
                      rows_out)

    for c in range(T // ATT_COMBINE_ROWS):
        rows = slice(c * ATT_COMBINE_ROWS, (c + 1) * ATT_COMBINE_ROWS)
        ls = [acc_l[d][rows, :] for d in ATT_DILATIONS]
        mx = functools.reduce(jnp.maximum, ls)
        es = [jnp.exp2(l - mx) for l in ls]
        num = functools.reduce(jnp.add, [e * acc_o[d][rows, :] for e, d in zip(es, ATT_DILATIONS)])
        o_ref[rows, :] = (num / functools.reduce(jnp.add, es)).astype(o_ref.dtype)

    for d in ATT_DILATIONS:
        n = T // d
        for r in range(d):
            row = r * (n + B)
            kc[d][row:row + B, :] = kc[d][row + n:row + n + B, :]
            vc[d][row:row + B, :] = vc[d][row + n:row + n + B, :]


def _late_chunk_spec(shape, steps, step_index):
    rows, cols = shape
    share = 1 if (rows // steps) % BF16_SUBLANES == 0 else 2
    assert rows % (steps // share) == 0 and (rows * share // steps) % BF16_SUBLANES == 0
    return pl.BlockSpec((rows * share // steps, cols), lambda *idx: (step_index(*idx) // share, 0))


def _attn(q, k, v, late, batch, seq):
    T = q.shape[0]
    tiles = seq // ATT_SUPER
    pairs = ATT_WIDTH // LANES
    spec = pl.BlockSpec((ATT_SUPER, LANES), lambda b, h, t: (b * tiles + t, h))
    late_specs = [_late_chunk_spec(w.shape, batch * pairs * tiles,
                                   lambda b, h, t: (b * pairs + h) * tiles + t) for w in late]
    rows = lambda n, dt: pltpu.VMEM((n, LANES), dt)
    kv_rows = [ATT_SUPER + d * ATT_BLOCK for d in ATT_DILATIONS]
    scratch = ([rows(ATT_SUPER, F32)] * 2 + [rows(ATT_SUPER, BF16) for _ in ATT_DILATIONS[1:]]
               + [rows(n, BF16) for n in kv_rows] + [rows(n, BF16) for n in kv_rows]
               + [rows(ATT_SUPER, F32) for _ in range(2 * len(ATT_DILATIONS))])
    res = pl.pallas_call(
        functools.partial(_attn_kernel, len(late)), grid=(batch, pairs, tiles),
        in_specs=[spec, spec, spec] + late_specs, out_specs=[spec] + late_specs,
        out_shape=[jax.ShapeDtypeStruct((T, ATT_WIDTH), BF16)]
        + [jax.ShapeDtypeStruct(w.shape, BF16) for w in late],
        scratch_shapes=scratch,
        compiler_params=pltpu.CompilerParams(
            dimension_semantics=("arbitrary", "arbitrary", "arbitrary"),
            vmem_limit_bytes=VMEM_LIMIT_BYTES),
        name="attn")(q, k, v, *late)
    return res[0], res[1:]


def _out_ffn_kernel(x1_ref, og_ref, oa_ref, ga_ref, wo_ref, g2_ref, w1_ref, w3_ref, w2_ref, gf_ref,
                    out_ref):
    groups = [slice(r, r + ROW_GROUP) for r in range(0, OUT_TOKEN_TILE, ROW_GROUP)]
    x2s = []
    for rows in groups:
        oa = _rms(oa_ref[rows, :].astype(F32), ga_ref[...]).astype(BF16)
        x2s.append(x1_ref[rows, :]
                   + jnp.dot(og_ref[rows, :], wo_ref[:GLA_WIDTH, :], preferred_element_type=F32)
                   + jnp.dot(oa, wo_ref[GLA_WIDTH:, :], preferred_element_type=F32))
    for rows, x2 in zip(groups, x2s):
        h = _rms(x2, g2_ref[...]).astype(BF16)
        x3 = x2 + 0.5 * _swiglu_half_step(h, w1_ref, w3_ref, w2_ref)
        out_ref[rows, :] = _rms(x3, gf_ref[...])


def _out_ffn(x1, og, oa, ga, wo, g2, w1, w3, w2, gf):
    T = x1.shape[0]
    tm = OUT_TOKEN_TILE
    row = lambda n: pl.BlockSpec((tm, n), lambda i: (i, 0))
    return pl.pallas_call(
        _out_ffn_kernel, grid=(T // tm,),
        in_specs=[row(D_MODEL), row(GLA_WIDTH), row(ATT_WIDTH), _const_spec(ga.shape),
                  _const_spec(wo.shape), _const_spec(g2.shape),
                  _const_spec(w1.shape), _const_spec(w3.shape), _const_spec(w2.shape),
                  _const_spec(gf.shape)],
        out_specs=row(D_MODEL),
        out_shape=jax.ShapeDtypeStruct((T, D_MODEL), F32),
        compiler_params=pltpu.CompilerParams(dimension_semantics=("arbitrary",),
                                             vmem_limit_bytes=VMEM_LIMIT_BYTES),
        name="out_ffn")(x1, og, oa, ga, wo, g2, w1, w3, w2, gf)


def _rope_tables():
    j = np.arange(LANES) % ATT_HD
    half = ROT_DIM // 2
    inv_freq = ROPE_THETA ** (-np.arange(0, ROT_DIM, 2, dtype=np.float32) / ROT_DIM)
    freq = np.where(j < ROT_DIM, inv_freq[j % half], 0.0).astype(np.float32)
    sgn1 = np.where(j < half, -1.0, 0.0).astype(np.float32)
    sgn2 = np.where((j >= half) & (j < ROT_DIM), 1.0, 0.0).astype(np.float32)
    return freq[None, :], sgn1[None, :], sgn2[None, :]


def _chunk_tril():
    i = np.arange(GLA_HALF)
    same = (i[:, None] // GLA_CHUNK) == (i[None, :] // GLA_CHUNK)
    return jnp.asarray((same & (i[None, :] <= i[:, None])).astype(np.float32), dtype=BF16)


def kernel(x, positions, ffn1_norm, ffn1_w1, ffn1_w3, ffn1_w2, mix_norm, w_in, gla_w_a2, gla_b_a,
           gla_out_norm, att_out_norm, w_out, ffn2_norm, ffn2_w1, ffn2_w3, ffn2_w2, final_norm):
    batch, seq, _ = x.shape
    T = batch * seq
    depth = ffn1_norm.shape[0]
    assert depth == 1, "the final norm is fused into the single layer's last kernel"
    freq, sgn1, sgn2 = (jnp.asarray(t) for t in _rope_tables())
    ltri = _chunk_tril()
    pos = positions.astype(F32).reshape(T // TOKEN_TILE, 1, TOKEN_TILE)
    xs = x.reshape(T, D_MODEL)

    o_gla_end = 2 * GLA_KW + 2 * GLA_WIDTH
    o_ga_end = o_gla_end + GLA_RANK
    for l in range(depth):
        wi = w_in[l]
        wgla = wi[:, :o_gla_end].astype(BF16)
        wga = jnp.pad(wi[:, o_gla_end:o_ga_end].astype(BF16), ((0, 0), (0, LANES - GLA_RANK)))
        watt = wi[:, o_ga_end:].astype(BF16)
        wa2 = jnp.pad(gla_w_a2[l].astype(BF16), ((0, LANES - GLA_RANK), (0, 0)))
        x1, gq, gk, gv, gr, la, aq, ak, av = _ffn_proj(
            xs, pos, ffn1_norm[l][None, :], ffn1_w1[l].astype(BF16), ffn1_w3[l].astype(BF16),
            ffn1_w2[l].astype(BF16), mix_norm[l][None, :], wgla, wga, wa2, gla_b_a[l][None, :], watt,
            freq, sgn1, sgn2)

        o_gla = _gla(gq, gk, gv, la, gr, gla_out_norm[l][None, :], ltri, batch)

        o_att, (w1b, w3b, w2b, wob) = _attn(
            aq, ak, av, (ffn2_w1[l], ffn2_w3[l], ffn2_w2[l], w_out[l]), batch, seq)

        xs = _out_ffn(x1, o_gla, o_att, att_out_norm[l][None, :], wob, ffn2_norm[l][None, :], w1b, w3b,
                      w2b, final_norm[None, :])
    return xs.reshape(batch, seq, D_MODEL)
```
